```python
import math
import jax, jax.numpy as jnp
from jax import lax
import numpy as np

D_MODEL = 1024
BATCH = 4
SEQ = 4096
DEPTH = 1

NORM_EPS = 1e-6
N_MOD = 6
MLA_HEADS = 8
MLA_Q_LORA = 768
MLA_KV_LORA = 256
MLA_NOPE = 64
MLA_ROPE = 32
MLA_QK = MLA_NOPE + MLA_ROPE
MLA_VDIM = 64
MLA_WIDTH = MLA_HEADS * MLA_VDIM
ROPE_THETA = 10000.0
ATTN_QBLOCK = 128
MOBA_HEADS = 8
MOBA_HDIM = 64
MOBA_WIDTH = MOBA_HEADS * MOBA_HDIM
MOBA_BLOCK = 256
MOBA_TOPK = 3
MOBA_QBLOCK = 32
D_MIX = MLA_WIDTH + MOBA_WIDTH
IN_WIDTHS = (MLA_Q_LORA, MLA_KV_LORA, MLA_ROPE, MOBA_WIDTH, MOBA_WIDTH, MOBA_WIDTH)
D_IN = sum(IN_WIDTHS)
REL_BUCKETS = 32
REL_MAX_DIST = 1024
PEER_NKEYS = 128
PEER_EXPERTS = PEER_NKEYS * PEER_NKEYS
PEER_HEADS = 8
PEER_TOPK = 16
PEER_DKEY = 128
PEER_HALF = PEER_DKEY // 2
PEER_TBLOCK = 128

kernel_name = "hymba_mla_moba_peer_adaln_layer"


def rmsnorm(x, g):
    xf = x.astype(jnp.float32)
    y = xf * lax.rsqrt(jnp.mean(xf * xf, axis=-1, keepdims=True) + NORM_EPS)
    return (y * g.astype(jnp.float32)).astype(x.dtype)


def rope(x, positions):
    half = x.shape[-1] // 2
    inv = ROPE_THETA ** (-jnp.arange(half, dtype=jnp.float32) / half)
    ang = positions.astype(jnp.float32)[:, :, None, None] * inv
    cos, sin = jnp.cos(ang), jnp.sin(ang)
    xf = x.astype(jnp.float32)
    x1, x2 = xf[..., :half], xf[..., half:]
    return jnp.concatenate([x1 * cos - x2 * sin, x2 * cos + x1 * sin], -1).astype(x.dtype)


def t5_bucket(dist):
    n = jnp.maximum(dist, 0)
    max_exact = REL_BUCKETS // 2
    nf = jnp.maximum(n, max_exact).astype(jnp.float32)
    large = max_exact + (jnp.log(nf / max_exact) / math.log(REL_MAX_DIST / max_exact)
                         * (REL_BUCKETS - max_exact)).astype(jnp.int32)
    large = jnp.minimum(large, REL_BUCKETS - 1)
    return jnp.where(n < max_exact, n, large)


def causal_attention(q, k, v):
    B, S, H, D = q.shape
    scale = D ** -0.5
    nq = S // ATTN_QBLOCK
    qb = q.reshape(B, nq, ATTN_QBLOCK, H, D).transpose(1, 0, 2, 3, 4)
    k_idx = jnp.arange(S)

    def one(args):
        qc, blk = args
        s = jnp.einsum('bqhd,bkhd->bhqk', qc, k).astype(jnp.float32) * scale
        q_idx = blk * ATTN_QBLOCK + jnp.arange(ATTN_QBLOCK)
        s = jnp.where(k_idx[None, :] <= q_idx[:, None], s, -jnp.inf)
        p = jax.nn.softmax(s, axis=-1).astype(v.dtype)
        return jnp.einsum('bhqk,bkhd->bqhd', p, v)

    o = lax.map(one, (qb, jnp.arange(nq)))
    return o.transpose(1, 0, 2, 3, 4).reshape(B, S, H, v.shape[-1])


def mla_group(c_q, c_kv, k_rope, positions, g_qa, w_uq, g_kva, w_ukv):
    B, S, _ = c_q.shape
    q = (rmsnorm(c_q, g_qa) @ w_uq).reshape(B, S, MLA_HEADS, MLA_QK)
    kv = (rmsnorm(c_kv, g_kva) @ w_ukv).reshape(B, S, MLA_HEADS, MLA_NOPE + MLA_VDIM)
    k_nope, v = kv[..., :MLA_NOPE], kv[..., MLA_NOPE:]
    q = jnp.concatenate([q[..., :MLA_NOPE], rope(q[..., MLA_NOPE:], positions)], -1)
    k_r = rope(k_rope[:, :, None, :], positions)
    k = jnp.concatenate([k_nope, jnp.broadcast_to(k_r, (B, S, MLA_HEADS, MLA_ROPE))], -1)
    return causal_attention(q, k, v).reshape(B, S, MLA_WIDTH)


def moba_group(q, k, v, rel_bias):
    B, S, _ = q.shape
    H, Dh, BS = MOBA_HEADS, MOBA_HDIM, MOBA_BLOCK
    q, k, v = (t.reshape(B, S, H, Dh).transpose(0, 2, 1, 3) for t in (q, k, v))
    nb = -(-S // BS)
    pad = nb * BS - S
    kb = jnp.pad(k, ((0, 0), (0, 0), (0, pad), (0, 0))).reshape(B, H, nb, BS, Dh)
    vb = jnp.pad(v, ((0, 0), (0, 0), (0, pad), (0, 0))).reshape(B, H, nb, BS, Dh)
    kmean = jnp.mean(kb, axis=3)
    n_sel = min(MOBA_TOPK, nb)
    scale = Dh ** -0.5
    bi = jnp.arange(B)[:, None, None, None]
    hi = jnp.arange(H)[None, :, None, None]
    hi5 = jnp.arange(H)[:, None, None, None]
    bias_hb = rel_bias.T.astype(jnp.float32)
    t_blk = jnp.arange(BS)
    nq = S // MOBA_QBLOCK
    qb = q.reshape(B, H, nq, MOBA_QBLOCK, Dh).transpose(2, 0, 1, 3, 4)

    def one(args):
        qc, blk = args
        q_idx = blk * MOBA_QBLOCK + jnp.arange(MOBA_QBLOCK)
        own = (blk * MOBA_QBLOCK) // BS
        gate = jnp.einsum('bhqd,bhnd->bhqn', qc, kmean).astype(jnp.float32)
        gate = jnp.where(jnp.arange(nb) < own, gate, -jnp.inf)
        _, sel = lax.top_k(gate, n_sel)
        slot_ok = jnp.arange(n_sel) < own
        k_sel = kb[bi, hi, sel]
        v_sel = vb[bi, hi, sel]
        s_sel = jnp.einsum('bhqd,bhqntd->bhqnt', qc, k_sel).astype(jnp.float32) * scale
        k_pos = sel[..., None] * BS + t_blk
        dist = q_idx[None, None, :, None, None] - k_pos
        s_sel = s_sel + bias_hb[hi5, t5_bucket(dist)]
        s_sel = jnp.where(slot_ok[:, None], s_sel, -jnp.inf)
        k_own = lax.dynamic_slice_in_dim(kb, own, 1, axis=2)[:, :, 0]
        v_own = lax.dynamic_slice_in_dim(vb, own, 1, axis=2)[:, :, 0]
        d_own = q_idx[:, None] - (own * BS + t_blk)[None, :]
        s_own = jnp.einsum('bhqd,bhtd->bhqt', qc, k_own).astype(jnp.float32) * scale
        s_own = s_own + bias_hb[:, t5_bucket(d_own)]
        s_own = jnp.where(d_own >= 0, s_own, -jnp.inf)
        s_all = jnp.concatenate([s_sel.reshape(B, H, MOBA_QBLOCK, n_sel * BS), s_own], -1)
        p = jax.nn.softmax(s_all, axis=-1).astype(v.dtype)
        p_sel = p[..., :n_sel * BS].reshape(B, H, MOBA_QBLOCK, n_sel, BS)
        p_own = p[..., n_sel * BS:]
        return (jnp.einsum('bhqnt,bhqntd->bhqd', p_sel, v_sel)
                + jnp.einsum('bhqt,bhtd->bhqd', p_own, v_own))

    o = lax.map(one, (qb, jnp.arange(nq)))
    o = o.transpose(1, 0, 3, 2, 4).reshape(B, S, H * Dh)
    return o


def peer_ffn(h, w_pq, sub_keys, u_tab, v_tab):
    B, S, D = h.shape
    K = PEER_TOPK
    q = (h @ w_pq).reshape(B, S, PEER_HEADS, 2, PEER_HALF)
    s = jnp.einsum('bshpd,hpnd->bshpn', q, sub_keys).astype(jnp.float32)
    s_top, i_top = lax.top_k(s, K)
    cand = (s_top[..., 0, :, None] + s_top[..., 1, None, :]).reshape(B, S, PEER_HEADS, K * K)
    cand_idx = (i_top[..., 0, :, None] * PEER_NKEYS + i_top[..., 1, None, :]).reshape(B, S, PEER_HEADS, K * K)
    best, pos = lax.top_k(cand, K)
    expert = jnp.take_along_axis(cand_idx, pos, axis=-1)
    gate = jax.nn.softmax(best, axis=-1).astype(h.dtype)
    nt = S // PEER_TBLOCK
    hc = h.reshape(B, nt, PEER_TBLOCK, D).transpose(1, 0, 2, 3)
    ec = expert.reshape(B, nt, PEER_TBLOCK, PEER_HEADS, K).transpose(1, 0, 2, 3, 4)
    gc = gate.reshape(B, nt, PEER_TBLOCK, PEER_HEADS, K).transpose(1, 0, 2, 3, 4)

    def one(args):
        hb, eb, gb = args
        a = jnp.einsum('btd,bthkd->bthk', hb, u_tab[eb])
        a = jax.nn.gelu(a, approximate=False) * gb
        return jnp.einsum('bthk,bthkd->btd', a, v_tab[eb])

    y = lax.map(one, (hc, ec, gc))
    return y.transpose(1, 0, 2, 3).reshape(B, S, D)


def setup_inputs(seed: int = 0) -> dict:
    key = jax.random.key(seed)
    ks = jax.random.split(key, 20)
    f32 = jnp.float32

    def nrm(k, shape, scale):
        return jax.random.normal(k, shape, f32) * scale

    def gain(k, shape):
        return 1.0 + 0.02 * jax.random.normal(k, shape, f32)

    L, D = DEPTH, D_MODEL
    offset = jax.random.randint(ks[2], (BATCH, 1), 0, 1024, dtype=jnp.int32)
    return {
        "x": nrm(ks[0], (BATCH, SEQ, D), 1.0),
        "c": nrm(ks[1], (BATCH, D), 1.0),
        "positions": (jnp.arange(SEQ, dtype=jnp.int32)[None, :] + offset).astype(jnp.int32),
        "w_ada": nrm(ks[3], (L, D, N_MOD * D), 0.5 * D ** -0.5),
        "b_ada": nrm(ks[4], (L, N_MOD * D), 0.02),
        "g_mix": gain(ks[5], (L, D)),
        "w_in": nrm(ks[6], (L, D, D_IN), D ** -0.5),
        "g_qa": gain(ks[7], (L, MLA_Q_LORA)),
        "w_uq": nrm(ks[8], (L, MLA_Q_LORA, MLA_HEADS * MLA_QK), MLA_Q_LORA ** -0.5),
        "g_kva": gain(ks[9], (L, MLA_KV_LORA)),
        "w_ukv": nrm(ks[10], (L, MLA_KV_LORA, MLA_HEADS * (MLA_NOPE + MLA_VDIM)), MLA_KV_LORA ** -0.5),
        "rel_bias": nrm(ks[11], (REL_BUCKETS, MOBA_HEADS), 0.5),
        "w_out": nrm(ks[12], (L, D_MIX, D), D_MIX ** -0.5),
        "g_ffn": gain(ks[13], (L, D)),
        "w_pq": nrm(ks[14], (L, D, PEER_HEADS * PEER_DKEY), D ** -0.5),
        "peer_keys": nrm(ks[15], (L, PEER_HEADS, 2, PEER_NKEYS, PEER_HALF), PEER_HALF ** -0.5),
        "peer_u": nrm(ks[16], (L, PEER_EXPERTS, D), D ** -0.5),
        "peer_v": nrm(ks[17], (L, PEER_EXPERTS, D), PEER_HEADS ** -0.5),
        "g_final": gain(ks[18], (D,)),
    }


def reference(x, c, positions, w_ada, b_ada, g_mix, w_in, g_qa, w_uq, g_kva, w_ukv,
              rel_bias, w_out, g_ffn, w_pq, peer_keys, peer_u, peer_v, g_final):
    split_at = [int(s) for s in np.cumsum(IN_WIDTHS)[:-1]]
    for l in range(DEPTH):
        mod = (jax.nn.silu(c) @ w_ada[l] + b_ada[l])[:, None, :]
        sh_a, sc_a, gt_a, sh_f, sc_f, gt_f = jnp.split(mod, N_MOD, axis=-1)
        h = rmsnorm(x, g_mix[l]) * (1.0 + sc_a) + sh_a
        z = h @ w_in[l]
        c_q, c_kv, k_rope, mq, mk, mv = jnp.split(z, split_at, axis=-1)
        o_mla = mla_group(c_q, c_kv, k_rope, positions, g_qa[l], w_uq[l], g_kva[l], w_ukv[l])
        o_moba = moba_group(mq, mk, mv, rel_bias)
        y = jnp.concatenate([o_mla, o_moba], axis=-1) @ w_out[l]
        x = x + gt_a * y
        h = rmsnorm(x, g_ffn[l]) * (1.0 + sc_f) + sh_f
        x = x + gt_f * peer_ffn(h, w_pq[l], peer_keys[l], peer_u[l], peer_v[l])
    return rmsnorm(x, g_final)
```

```python
import functools
import math

import jax
import jax.numpy as jnp
import numpy as np
from jax import lax
from jax.experimental import pallas as pl
from jax.experimental.pallas import tpu as pltpu

F32 = jnp.float32
BF16 = jnp.bfloat16
HIGHEST = lax.Precision.HIGHEST

D_MODEL = 1024
N_MOD = 6
NORM_EPS = 1e-6
MLA_HEADS = 8
MLA_Q_LORA = 768
MLA_KV_LORA = 256
MLA_NOPE = 64
MLA_ROPE = 32
MLA_VDIM = 64
ROPE_THETA = 10000.0
MOBA_HEADS = 8
MOBA_HDIM = 64
MOBA_BLOCK = 256
MOBA_TOPK = 3
REL_BUCKETS = 32
REL_MAX_DIST = 1024
PEER_NKEYS = 128
PEER_HEADS = 8
PEER_TOPK = 16
PEER_HALF = 64

LANE = 128
HEAD_PAD = 128
ROPE_LO = MLA_NOPE
ROPE_HALF = MLA_ROPE // 2
MASK_NEG = -1e9
VMEM_LIMIT = 56 * 1024 * 1024
TOK_TILE = MOBA_BLOCK
MOBA_MAX_BLOCKS = LANE // MOBA_HEADS
PEER_TM = 512
PEER_TE = 1024


def _cparams(sem):
    return pltpu.CompilerParams(dimension_semantics=sem, vmem_limit_bytes=VMEM_LIMIT)


def _rms(x, g):
    ms = jnp.mean(x * x, axis=-1, keepdims=True)
    return x * lax.rsqrt(ms + NORM_EPS) * g


def _dot_nt(a, b, precision=None):
    return lax.dot_general(a, b, (((1,), (1,)), ((), ())), precision=precision,
                           preferred_element_type=F32)


def _ada_kernel(c_ref, w_ref, b_ref, o_ref):
    c = c_ref[...]
    s = c / (1.0 + jnp.exp(-c))
    o_ref[...] = jnp.dot(s, w_ref[...], precision=HIGHEST, preferred_element_type=F32) + b_ref[...]


def _ada(c, w, b):
    bsz, d = c.shape
    n = w.shape[1]
    return pl.pallas_call(
        _ada_kernel,
        out_shape=jax.ShapeDtypeStruct((bsz, n), F32),
        grid=(n // d,),
        in_specs=[pl.BlockSpec((bsz, d), lambda j: (0, 0)),
                  pl.BlockSpec((d, d), lambda j: (0, j)),
                  pl.BlockSpec((1, d), lambda j: (0, j))],
        out_specs=pl.BlockSpec((bsz, d), lambda j: (0, j)),
        compiler_params=_cparams(("arbitrary",)),
        name="ada",
    )(c, w, b.reshape(1, n))


def _rope_block(x, cosm, sinm, lane):
    up = pltpu.roll(x, LANE - ROPE_HALF, 1)
    dn = pltpu.roll(x, ROPE_HALF, 1)
    rot = jnp.where(lane < ROPE_LO + ROPE_HALF, -up, dn)
    return x * cosm + rot * sinm


def _inproj_kernel(blocks_per_seq,
                   x_ref, sc_ref, sh_ref, gmix_ref, win_ref, gqa_ref, wuq_ref, gkva_ref,
                   wk_ref, wv_ref, pos_ref, inv_ref,
                   qmla_ref, kmla_ref, vmla_ref, mq_ref, mk_ref, mv_ref,
                   kmt_ref):
    i = pl.program_id(0)
    own = i % blocks_per_seq
    tm = x_ref.shape[0]

    @pl.when(i == 0)
    def _():
        kmt_ref[...] = jnp.zeros_like(kmt_ref)

    h = _rms(x_ref[...], gmix_ref[...]) * (1.0 + sc_ref[...]) + sh_ref[...]
    z = jnp.dot(h.astype(BF16), win_ref[...], preferred_element_type=F32)
    o = 0
    c_q = z[:, o:o + MLA_Q_LORA]; o += MLA_Q_LORA
    c_kv = z[:, o:o + MLA_KV_LORA]; o += MLA_KV_LORA
    k_r = z[:, o:o + HEAD_PAD]; o += HEAD_PAD
    mq = z[:, o:o + MOBA_HEADS * HEAD_PAD]; o += MOBA_HEADS * HEAD_PAD
    mk = z[:, o:o + MOBA_HEADS * HEAD_PAD]; o += MOBA_HEADS * HEAD_PAD
    mv = z[:, o:o + MOBA_HEADS * MOBA_HDIM]

    lane = lax.broadcasted_iota(jnp.int32, (tm, LANE), 1)
    in_rope = (lane >= ROPE_LO) & (lane < ROPE_LO + MLA_ROPE)
    ang = pos_ref[...].astype(F32) * inv_ref[...]
    cosm = jnp.where(in_rope, jnp.cos(ang), 1.0)
    sinm = jnp.where(in_rope, jnp.sin(ang), 0.0)

    qn = _rms(c_q, gqa_ref[...]).astype(BF16)
    q = jnp.dot(qn, wuq_ref[...], preferred_element_type=F32)
    q_scale = (MLA_NOPE + MLA_ROPE) ** -0.5
    kvn = _rms(c_kv, gkva_ref[...]).astype(BF16)
    kn = jnp.dot(kvn, wk_ref[...], preferred_element_type=F32)
    k_rr = _rope_block(k_r, cosm, sinm, lane)
    for hd in range(MLA_HEADS):
        sl = slice(hd * HEAD_PAD, (hd + 1) * HEAD_PAD)
        qmla_ref[:, sl] = (_rope_block(q[:, sl], cosm, sinm, lane) * q_scale).astype(BF16)
        kmla_ref[:, sl] = (kn[:, sl] + k_rr).astype(BF16)
    vmla_ref[...] = jnp.dot(kvn, wv_ref[...], preferred_element_type=F32).astype(BF16)

    gate_t = _dot_nt(kmt_ref[...], mq, precision=HIGHEST)
    nblk = kmt_ref.shape[0] // MOBA_HEADS
    row = lax.broadcasted_iota(jnp.int32, (nblk, tm), 0).astype(F32)
    valid = row < own.astype(F32)
    bias_rows = []
    for hd in range(MOBA_HEADS):
        g = jnp.where(valid, gate_t[hd * nblk:(hd + 1) * nblk, :], -jnp.inf)
        bias = jnp.full((nblk, tm), MASK_NEG, F32)
        for _ in range(MOBA_TOPK):
            m = jnp.max(g, axis=0, keepdims=True)
            first = jnp.min(jnp.where(g == m, row, float(nblk)), axis=0, keepdims=True)
            pick = (row == first) & (m > -jnp.inf)
            bias = jnp.where(pick, 0.0, bias)
            g = jnp.where(pick, -jnp.inf, g)
        bias_rows.append(bias)
    sel_tm = jnp.concatenate(bias_rows, axis=0).T
    in_sel = (lane >= MOBA_HDIM) & (lane < MOBA_HDIM + nblk)
    m_scale = MOBA_HDIM ** -0.5
    for hd in range(MOBA_HEADS):
        sl = slice(hd * HEAD_PAD, (hd + 1) * HEAD_PAD)
        placed = pltpu.roll(sel_tm, (MOBA_HDIM - nblk * hd) % LANE, 1)
        mq_ref[:, sl] = jnp.where(in_sel, placed, mq[:, sl] * m_scale).astype(BF16)
    mk_ref[...] = mk.astype(BF16)
    mv_ref[...] = mv.astype(BF16)

    kmean = jnp.mean(mk, axis=0, keepdims=True)
    lane_k = lax.broadcasted_iota(jnp.int32, kmean.shape, 1)
    for hd in range(MOBA_HEADS):
        in_head = (lane_k >= hd * HEAD_PAD) & (lane_k < (hd + 1) * HEAD_PAD)
        kmt_ref[pl.ds(hd * nblk + own, 1), :] = jnp.where(in_head, kmean, 0.0)


def _inproj(x2, sc, sh, g_mix, w_in_p, g_qa, w_uq_p, g_kva, w_k_p, w_v, pos, inv_row, seq):
    n, d = x2.shape
    tm = TOK_TILE
    bps = seq // tm
    assert seq // MOBA_BLOCK <= MOBA_MAX_BLOCKS
    full = lambda a: pl.BlockSpec(a.shape, lambda i: (0,) * a.ndim)
    per_b = pl.BlockSpec((None, 1, d), lambda i: (i // bps, 0, 0))
    tok = lambda w: pl.BlockSpec((tm, w), lambda i: (i, 0))
    widths = (MLA_HEADS * HEAD_PAD, MLA_HEADS * HEAD_PAD, MLA_HEADS * MLA_VDIM,
              MOBA_HEADS * HEAD_PAD, MOBA_HEADS * HEAD_PAD, MOBA_HEADS * MOBA_HDIM)
    return pl.pallas_call(
        functools.partial(_inproj_kernel, bps),
        out_shape=[jax.ShapeDtypeStruct((n, w), BF16) for w in widths],
        grid=(n // tm,),
        in_specs=[tok(d), per_b, per_b, full(g_mix), full(w_in_p), full(g_qa), full(w_uq_p),
                  full(g_kva), full(w_k_p), full(w_v), tok(1), full(inv_row)],
        out_specs=[tok(w) for w in widths],
        scratch_shapes=[pltpu.VMEM((MOBA_HEADS * MOBA_MAX_BLOCKS, MOBA_HEADS * HEAD_PAD), F32)],
        compiler_params=_cparams(("arbitrary",)),
        name="inproj",
    )(x2, sc, sh, g_mix, w_in_p, g_qa, w_uq_p, g_kva, w_k_p, w_v, pos, inv_row)


def _attn_kernel(moba, q_ref, k_ref, v_ref, *rest):
    if moba:
        t_ref, o_ref = rest
    else:
        (o_ref,) = rest
    bq = pl.program_id(2)
    tq = q_ref.shape[0]
    vd = v_ref.shape[1] // 2
    r_i = lax.broadcasted_iota(jnp.int32, (tq, tq), 0)
    c_i = lax.broadcasted_iota(jnp.int32, (tq, tq), 1)
    causal = r_i >= c_i
    lane_k = lax.broadcasted_iota(jnp.int32, (1, HEAD_PAD), 1)
    outs = []
    for hh in range(2):
        sl = slice(hh * HEAD_PAD, (hh + 1) * HEAD_PAD)
        q = q_ref[:, sl]

        def scores(n, diag):
            k = k_ref[pl.ds(pl.multiple_of(n * tq, tq), tq), sl]
            if moba and not diag:
                k = k + jnp.where(lane_k == MOBA_HDIM + n, 1.0, 0.0).astype(BF16)
            s = _dot_nt(q, k)
            if moba:
                s = s + t_ref[hh, bq - n]
            if diag:
                s = jnp.where(causal, s, MASK_NEG)
            return s

        def update(n, carry, diag):
            m, l, acc = carry
            s = scores(n, diag)
            m_new = jnp.maximum(m, jnp.max(s, axis=-1, keepdims=True))
            alpha = jnp.exp(m - m_new)
            p = jnp.exp(s - m_new)
            l = alpha * l + jnp.sum(p, axis=-1, keepdims=True)
            v = v_ref[pl.ds(pl.multiple_of(n * tq, tq), tq), :]
            acc = alpha * acc + jnp.dot(p.astype(BF16), v, preferred_element_type=F32)
            return m_new, l, acc

        init = (jnp.full((tq, 1), -jnp.inf, F32), jnp.zeros((tq, 1), F32),
                jnp.zeros((tq, v_ref.shape[1]), F32))
        carry = update(bq, init, True)
        carry = lax.fori_loop(0, bq, lambda n, c: update(n, c, False), carry)
        m, l, acc = carry
        outs.append(acc / l)
    lane_o = lax.broadcasted_iota(jnp.int32, outs[0].shape, 1)
    o_ref[...] = jnp.where(lane_o < vd, outs[0], outs[1]).astype(o_ref.dtype)


def _attn(q, k, v, seq, bias_tab=None):
    n = q.shape[0]
    bsz = n // seq
    heads = q.shape[1] // HEAD_PAD
    tq = MOBA_BLOCK
    nq = seq // tq
    moba = bias_tab is not None
    in_specs = [pl.BlockSpec((tq, 2 * HEAD_PAD), lambda b, p, i: (b * nq + i, p)),
                pl.BlockSpec((seq, 2 * HEAD_PAD), lambda b, p, i: (b, p)),
                pl.BlockSpec((seq, HEAD_PAD), lambda b, p, i: (b, p))]
    args = [q, k, v]
    if moba:
        in_specs.append(pl.BlockSpec((2,) + bias_tab.shape[1:], lambda b, p, i: (p, 0, 0, 0)))
        args.append(bias_tab)
    return pl.pallas_call(
        functools.partial(_attn_kernel, moba),
        out_shape=jax.ShapeDtypeStruct((n, v.shape[1]), BF16),
        grid=(bsz, heads // 2, nq),
        in_specs=in_specs,
        out_specs=pl.BlockSpec((tq, HEAD_PAD), lambda b, p, i: (b * nq + i, p)),
        compiler_params=_cparams(("arbitrary", "arbitrary", "arbitrary")),
        name="moba_attn" if moba else "mla_attn",
    )(*args)


def _top16(s, v_ref, r_ref=None):
    r = jnp.full(s.shape, float(PEER_TOPK), F32)
    for a in range(PEER_TOPK):
        m = jnp.max(s, axis=0, keepdims=True)
        hit = s == m
        if r_ref is not None:
            r = jnp.where(hit, float(a), r)
        s = jnp.where(hit, -jnp.inf, s)
        v_ref[a:a + 1, :] = m
    if r_ref is not None:
        r_ref[...] = r


def _mid_kernel(x_ref, oa_ref, ob_ref, wo_ref, gta_ref, scf_ref, shf_ref, gffn_ref, wpq_ref,
                kpad_ref, x1_ref, h2_ref, r1_ref, e1_ref, k0_ref, e0_ref, v0_s, v1_s):
    half = oa_ref.shape[1]
    y = (jnp.dot(oa_ref[...], wo_ref[:half, :], preferred_element_type=F32)
         + jnp.dot(ob_ref[...], wo_ref[half:, :], preferred_element_type=F32))
    x1 = x_ref[...] + gta_ref[...] * y
    x1_ref[...] = x1
    h2 = (_rms(x1, gffn_ref[...]) * (1.0 + scf_ref[...]) + shf_ref[...]).astype(BF16)
    h2_ref[...] = h2
    pq = jnp.dot(h2, wpq_ref[...], preferred_element_type=F32).astype(BF16)
    nk = PEER_NKEYS
    for hd in range(PEER_HEADS):
        st = _dot_nt(kpad_ref[hd], pq[:, hd * LANE:(hd + 1) * LANE])
        s0 = st[:nk, :]
        s1 = st[nk:, :]
        _top16(s0, v0_s)
        _top16(s1, v1_s, r1_ref.at[hd])
        v0 = v0_s[...]
        v1 = v1_s[...]
        groups = [v0[0:1, :] + v1[0:8, :], v0[0:1, :] + v1[8:16, :]]
        groups += [v0[a:a + 1, :] + v1[0:8, :] for a in range(1, 8)]
        groups.append(v0[8:16, :] + v1[0:1, :])
        top = groups[0][0:1, :]
        work = list(groups)
        thr = top
        for _ in range(PEER_TOPK):
            thr = functools.reduce(jnp.maximum, [jnp.max(g, axis=0, keepdims=True) for g in work])
            work = [jnp.where(g == thr, -jnp.inf, g) for g in work]
        z = functools.reduce(
            jnp.add, [jnp.sum(jnp.where(g >= thr, jnp.exp(g - top), 0.0), axis=0, keepdims=True)
                      for g in groups])
        cnt = jnp.zeros(s0.shape, F32)
        for b in range(PEER_TOPK):
            cnt = cnt + jnp.where(s0 + v1[b:b + 1, :] >= thr, 1.0, 0.0)
        k0_ref[hd] = cnt
        e0_ref[hd] = jnp.exp(s0 - v0[0:1, :]) / z
        e1_ref[hd] = jnp.exp(s1 - v1[0:1, :])


def _mid(x2, o_mla, o_moba, w_out, gta, scf, shf, g_ffn, w_pq, kpad, seq):
    n, d = x2.shape
    tm = TOK_TILE
    bps = seq // tm
    full = lambda a: pl.BlockSpec(a.shape, lambda i: (0,) * a.ndim)
    per_b = pl.BlockSpec((None, 1, d), lambda i: (i // bps, 0, 0))
    tok = lambda w: pl.BlockSpec((tm, w), lambda i: (i, 0))
    sel_shape = jax.ShapeDtypeStruct((PEER_HEADS, PEER_NKEYS, n), F32)
    sel_spec = pl.BlockSpec((PEER_HEADS, PEER_NKEYS, tm), lambda i: (0, 0, i))
    return pl.pallas_call(
        _mid_kernel,
        out_shape=[jax.ShapeDtypeStruct((n, d), F32), jax.ShapeDtypeStruct((n, d), BF16),
                   sel_shape, sel_shape, sel_shape, sel_shape],
        grid=(n // tm,),
        in_specs=[tok(d), tok(o_mla.shape[1]), tok(o_moba.shape[1]), full(w_out), per_b, per_b,
                  per_b, full(g_ffn), full(w_pq), full(kpad)],
        out_specs=[tok(d), tok(d), sel_spec, sel_spec, sel_spec, sel_spec],
        scratch_shapes=[pltpu.VMEM((PEER_TOPK, tm), F32), pltpu.VMEM((PEER_TOPK, tm), F32)],
        compiler_params=_cparams(("arbitrary",)),
        name="mid",
    )(x2, o_mla, o_moba, w_out, gta, scf, shf, g_ffn, w_pq, kpad)


def _peer_kernel(h_ref, u_ref, vt_ref, r1_ref, e1_ref, k0_ref, e0_ref, x1_ref, gtf_ref, gfin_ref,
                 o_ref, acc_ref, at_ref, p_ref):
    e = pl.program_id(1)
    te, tm = at_ref.shape
    groups = te // PEER_NKEYS
    ltiles = tm // LANE

    @pl.when(e == 0)
    def _():
        acc_ref[...] = jnp.zeros_like(acc_ref)

    at_ref[...] = _dot_nt(u_ref[...], h_ref[...])

    def lane_tile(lt, carry):
        cols = pl.ds(pl.multiple_of(lt * LANE, LANE), LANE)
        for ig in range(groups):
            rows = slice(ig * PEER_NKEYS, (ig + 1) * PEER_NKEYS)
            a = at_ref[rows, cols]
            gelu = 0.5 * a * (1.0 + lax.erf(a * (1.0 / math.sqrt(2.0))))
            w = jnp.zeros_like(a)
            for hd in range(PEER_HEADS):
                k0 = k0_ref[hd, ig:ig + 1, cols]
                e0 = e0_ref[hd, ig:ig + 1, cols]
                w = w + jnp.where(r1_ref[hd, :, cols] < k0, e1_ref[hd, :, cols], 0.0) * e0
            p_ref[rows, cols] = (gelu * w).astype(BF16)
        return carry

    lax.fori_loop(0, ltiles, lane_tile, 0)
    acc_ref[...] += jnp.dot(vt_ref[...], p_ref[...], preferred_element_type=F32)

    @pl.when(e == pl.num_programs(1) - 1)
    def _():
        x2 = x1_ref[...] + gtf_ref[...] * acc_ref[...].T
        o_ref[...] = _rms(x2, gfin_ref[...])


def _peer(h2, u_b, vt_b, r1, e1, k0, e0, x1, gtf, g_fin, seq):
    n, d = h2.shape
    ne = u_b.shape[0]
    tm, te = PEER_TM, PEER_TE
    bps = seq // tm
    sel_spec = pl.BlockSpec((PEER_HEADS, PEER_NKEYS, tm), lambda t, e: (0, 0, t))
    key_spec = pl.BlockSpec((PEER_HEADS, te // PEER_NKEYS, tm), lambda t, e: (0, e, t))
    return pl.pallas_call(
        _peer_kernel,
        out_shape=jax.ShapeDtypeStruct((n, d), F32),
        grid=(n // tm, ne // te),
        in_specs=[pl.BlockSpec((tm, d), lambda t, e: (t, 0)),
                  pl.BlockSpec((te, d), lambda t, e: (e, 0)),
                  pl.BlockSpec((d, te), lambda t, e: (0, e)),
                  sel_spec, sel_spec, key_spec, key_spec,
                  pl.BlockSpec((tm, d), lambda t, e: (t, 0)),
                  pl.BlockSpec((None, 1, d), lambda t, e: (t // bps, 0, 0)),
                  pl.BlockSpec((1, d), lambda t, e: (0, 0))],
        out_specs=pl.BlockSpec((tm, d), lambda t, e: (t, 0)),
        scratch_shapes=[pltpu.VMEM((d, tm), F32), pltpu.VMEM((te, tm), F32),
                        pltpu.VMEM((te, tm), BF16)],
        compiler_params=_cparams(("arbitrary", "arbitrary")),
        name="peer",
    )(h2, u_b, vt_b, r1, e1, k0, e0, x1, gtf, g_fin)


def _pad_heads(w, heads, width):
    k = w.shape[0]
    w = w.reshape(k, heads, width)
    return jnp.pad(w, ((0, 0), (0, 0), (0, HEAD_PAD - width))).reshape(k, heads * HEAD_PAD)


def _rel_bias_table(rel_bias, nblk):
    bs = MOBA_BLOCK
    dist = jnp.arange(-bs, nblk * bs, dtype=jnp.int32)
    nn = jnp.maximum(dist, 0)
    max_exact = REL_BUCKETS // 2
    nf = jnp.maximum(nn, max_exact).astype(F32)
    large = max_exact + (jnp.log(nf / max_exact) / math.log(REL_MAX_DIST / max_exact)
                         * (REL_BUCKETS - max_exact)).astype(jnp.int32)
    large = jnp.minimum(large, REL_BUCKETS - 1)
    bucket = jnp.where(nn < max_exact, nn, large)
    f = rel_bias.astype(F32)[bucket].T
    heads = f.shape[0]
    idx = (bs * jnp.arange(nblk)[:, None] + (bs - 1) - jnp.arange(2 * bs)[None, :]) + bs
    w = f[:, idx]
    rows = jnp.tile(w, (1, 1, bs))[:, :, :bs * (2 * bs - 1)].reshape(heads, nblk, bs, 2 * bs - 1)
    return rows[:, :, :, bs - 1:]


def kernel(x, c, positions, w_ada, b_ada, g_mix, w_in, g_qa, w_uq, g_kva, w_ukv, rel_bias,
           w_out, g_ffn, w_pq, peer_keys, peer_u, peer_v, g_final):
    bsz, seq, d = x.shape
    n = bsz * seq
    nblk = seq // MOBA_BLOCK
    x2 = x.reshape(n, d)
    assert w_ada.shape[0] == 1, "single-layer block: the final norm is fused into the peer kernel"
    for l in range(1):
        mod = _ada(c, w_ada[l], b_ada[l]).reshape(bsz, N_MOD, 1, d)
        sh_a, sc_a, gt_a, sh_f, sc_f, gt_f = (mod[:, j] for j in range(N_MOD))

        cq, ckv, kr, mq, mk, mv = jnp.split(w_in[l], np.cumsum(
            [MLA_Q_LORA, MLA_KV_LORA, MLA_ROPE, MOBA_HEADS * MOBA_HDIM, MOBA_HEADS * MOBA_HDIM])
            .tolist(), axis=1)
        kr_p = jnp.pad(kr, ((0, 0), (ROPE_LO, HEAD_PAD - ROPE_LO - MLA_ROPE)))
        w_in_p = jnp.concatenate(
            [cq, ckv, kr_p, _pad_heads(mq, MOBA_HEADS, MOBA_HDIM),
             _pad_heads(mk, MOBA_HEADS, MOBA_HDIM), mv], axis=1).astype(BF16)
        w_uq_p = _pad_heads(w_uq[l], MLA_HEADS, MLA_NOPE + MLA_ROPE).astype(BF16)
        w_kv = w_ukv[l].reshape(MLA_KV_LORA, MLA_HEADS, MLA_NOPE + MLA_VDIM)
        w_k_p = _pad_heads(w_kv[:, :, :MLA_NOPE].reshape(MLA_KV_LORA, -1), MLA_HEADS,
                           MLA_NOPE).astype(BF16)
        w_v = w_kv[:, :, MLA_NOPE:].reshape(MLA_KV_LORA, -1).astype(BF16)
        inv = ROPE_THETA ** (-jnp.arange(ROPE_HALF, dtype=F32) / ROPE_HALF)
        inv_row = jnp.zeros((1, LANE), F32).at[0, ROPE_LO:ROPE_LO + MLA_ROPE].set(
            jnp.concatenate([inv, inv]))
        pos = positions.reshape(n, 1)

        q_mla, k_mla, v_mla, q_mb, k_mb, v_mb = _inproj(
            x2, sc_a, sh_a, g_mix[l].reshape(1, d), w_in_p, g_qa[l].reshape(1, -1), w_uq_p,
            g_kva[l].reshape(1, -1), w_k_p, w_v, pos, inv_row, seq)

        o_mla = _attn(q_mla, k_mla, v_mla, seq)
        o_moba = _attn(q_mb, k_mb, v_mb, seq, _rel_bias_table(rel_bias, nblk))

        keys = peer_keys[l]
        zer = jnp.zeros_like(keys[:, 0])
        kpad = jnp.concatenate([jnp.concatenate([keys[:, 0], zer], axis=2),
                                jnp.concatenate([zer, keys[:, 1]], axis=2)], axis=1).astype(BF16)
        x1, h2, r1, e1, k0, e0 = _mid(
            x2, o_mla, o_moba, w_out[l].astype(BF16), gt_a, sc_f, sh_f, g_ffn[l].reshape(1, d),
            w_pq[l].astype(BF16), kpad, seq)

        x2 = _peer(h2, peer_u[l].astype(BF16), peer_v[l].T.astype(BF16), r1, e1, k0, e0, x1,
                   gt_f, g_final.reshape(1, d), seq)
    return x2.reshape(bsz, seq, d)
```

```python
import functools
import math

import jax
import jax.numpy as jnp
import numpy as np
from jax import lax
from jax.experimental import pallas as pl
from jax.experimental.pallas import tpu as pltpu

F32 = jnp.float32
BF16 = jnp.bfloat16
HIGHEST = lax.Precision.HIGHEST

D_MODEL = 1024
N_MOD = 6
NORM_EPS = 1e-6
MLA_HEADS = 8
MLA_Q_LORA = 768
MLA_KV_LORA = 256
MLA_NOPE = 64
MLA_ROPE = 32
MLA_VDIM = 64
ROPE_THETA = 10000.0
MOBA_HEADS = 8
MOBA_HDIM = 64
MOBA_BLOCK = 256
MOBA_TOPK = 3
REL_BUCKETS = 32
REL_MAX_DIST = 1024
PEER_NKEYS = 128
PEER_HEADS = 8
PEER_TOPK = 16
PEER_HALF = 64

LANE = 128
HEAD_PAD = 128
ROPE_LO = MLA_NOPE
ROPE_HALF = MLA_ROPE // 2
MASK_NEG = -1e9
VMEM_LIMIT = 56 * 1024 * 1024
TOK_TILE = MOBA_BLOCK
MOBA_MAX_BLOCKS = LANE // MOBA_HEADS
PEER_TM = 512
PEER_TE = 1024
PEER_SLAB = 256
PEER_JCHUNK = 64


def _cparams(sem):
    return pltpu.CompilerParams(dimension_semantics=sem, vmem_limit_bytes=VMEM_LIMIT)


def _rms(x, g):
    ms = jnp.mean(x * x, axis=-1, keepdims=True)
    return x * lax.rsqrt(ms + NORM_EPS) * g


def _dot_nt(a, b, precision=None):
    return lax.dot_general(a, b, (((1,), (1,)), ((), ())), precision=precision,
                           preferred_element_type=F32)


def _ada_kernel(c_ref, w_ref, b_ref, o_ref):
    c = c_ref[...]
    s = c / (1.0 + jnp.exp(-c))
    o_ref[...] = jnp.dot(s, w_ref[...], precision=HIGHEST, preferred_element_type=F32) + b_ref[...]


def _ada(c, w, b):
    bsz, d = c.shape
    n = w.shape[1]
    return pl.pallas_call(
        _ada_kernel,
        out_shape=jax.ShapeDtypeStruct((bsz, n), F32),
        grid=(n // d,),
        in_specs=[pl.BlockSpec((bsz, d), lambda j: (0, 0)),
                  pl.BlockSpec((d, d), lambda j: (0, j)),
                  pl.BlockSpec((1, d), lambda j: (0, j))],
        out_specs=pl.BlockSpec((bsz, d), lambda j: (0, j)),
        compiler_params=_cparams(("arbitrary",)),
        name="ada",
    )(c, w, b.reshape(1, n))


def _rope_block(x, cosm, sinm, lane):
    up = pltpu.roll(x, LANE - ROPE_HALF, 1)
    dn = pltpu.roll(x, ROPE_HALF, 1)
    rot = jnp.where(lane < ROPE_LO + ROPE_HALF, -up, dn)
    return x * cosm + rot * sinm


def _inproj_kernel(blocks_per_seq,
                   x_ref, sc_ref, sh_ref, gmix_ref, win_ref, gqa_ref, wuq_ref, gkva_ref,
                   wk_ref, wv_ref, pos_ref, inv_ref,
                   qmla_ref, kmla_ref, vmla_ref, mq_ref, mk_ref, mv_ref,
                   kmt_ref):
    i = pl.program_id(0)
    own = i % blocks_per_seq
    tm = x_ref.shape[0]

    @pl.when(i == 0)
    def _():
        kmt_ref[...] = jnp.zeros_like(kmt_ref)

    h = _rms(x_ref[...], gmix_ref[...]) * (1.0 + sc_ref[...]) + sh_ref[...]
    z = jnp.dot(h.astype(BF16), win_ref[...], preferred_element_type=F32)
    o = 0
    c_q = z[:, o:o + MLA_Q_LORA]; o += MLA_Q_LORA
    c_kv = z[:, o:o + MLA_KV_LORA]; o += MLA_KV_LORA
    k_r = z[:, o:o + HEAD_PAD]; o += HEAD_PAD
    mq = z[:, o:o + MOBA_HEADS * HEAD_PAD]; o += MOBA_HEADS * HEAD_PAD
    mk = z[:, o:o + MOBA_HEADS * HEAD_PAD]; o += MOBA_HEADS * HEAD_PAD
    mv = z[:, o:o + MOBA_HEADS * MOBA_HDIM]

    lane = lax.broadcasted_iota(jnp.int32, (tm, LANE), 1)
    in_rope = (lane >= ROPE_LO) & (lane < ROPE_LO + MLA_ROPE)
    ang = pos_ref[...].astype(F32) * inv_ref[...]
    cosm = jnp.where(in_rope, jnp.cos(ang), 1.0)
    sinm = jnp.where(in_rope, jnp.sin(ang), 0.0)

    qn = _rms(c_q, gqa_ref[...]).astype(BF16)
    q = jnp.dot(qn, wuq_ref[...], preferred_element_type=F32)
    q_scale = (MLA_NOPE + MLA_ROPE) ** -0.5
    kvn = _rms(c_kv, gkva_ref[...]).astype(BF16)
    kn = jnp.dot(kvn, wk_ref[...], preferred_element_type=F32)
    k_rr = _rope_block(k_r, cosm, sinm, lane)
    for hd in range(MLA_HEADS):
        sl = slice(hd * HEAD_PAD, (hd + 1) * HEAD_PAD)
        qmla_ref[:, sl] = (_rope_block(q[:, sl], cosm, sinm, lane) * q_scale).astype(BF16)
        kmla_ref[:, sl] = (kn[:, sl] + k_rr).astype(BF16)
    vmla_ref[...] = jnp.dot(kvn, wv_ref[...], preferred_element_type=F32).astype(BF16)

    gate_t = _dot_nt(kmt_ref[...], mq, precision=HIGHEST)
    nblk = kmt_ref.shape[0] // MOBA_HEADS
    row = lax.broadcasted_iota(jnp.int32, (nblk, tm), 0).astype(F32)
    valid = row < own.astype(F32)
    bias_rows = []
    for hd in range(MOBA_HEADS):
        g = jnp.where(valid, gate_t[hd * nblk:(hd + 1) * nblk, :], -jnp.inf)
        bias = jnp.full((nblk, tm), MASK_NEG, F32)
        for _ in range(MOBA_TOPK):
            m = jnp.max(g, axis=0, keepdims=True)
            first = jnp.min(jnp.where(g == m, row, float(nblk)), axis=0, keepdims=True)
            pick = (row == first) & (m > -jnp.inf)
            bias = jnp.where(pick, 0.0, bias)
            g = jnp.where(pick, -jnp.inf, g)
        bias_rows.append(bias)
    sel_tm = jnp.concatenate(bias_rows, axis=0).T
    in_sel = (lane >= MOBA_HDIM) & (lane < MOBA_HDIM + nblk)
    m_scale = MOBA_HDIM ** -0.5
    for hd in range(MOBA_HEADS):
        sl = slice(hd * HEAD_PAD, (hd + 1) * HEAD_PAD)
        placed = pltpu.roll(sel_tm, (MOBA_HDIM - nblk * hd) % LANE, 1)
        mq_ref[:, sl] = jnp.where(in_sel, placed, mq[:, sl] * m_scale).astype(BF16)
    mk_ref[...] = mk.astype(BF16)
    mv_ref[...] = mv.astype(BF16)

    kmean = jnp.mean(mk, axis=0, keepdims=True)
    lane_k = lax.broadcasted_iota(jnp.int32, kmean.shape, 1)
    for hd in range(MOBA_HEADS):
        in_head = (lane_k >= hd * HEAD_PAD) & (lane_k < (hd + 1) * HEAD_PAD)
        kmt_ref[pl.ds(hd * nblk + own, 1), :] = jnp.where(in_head, kmean, 0.0)


def _inproj(x2, sc, sh, g_mix, w_in_p, g_qa, w_uq_p, g_kva, w_k_p, w_v, pos, inv_row, seq):
    n, d = x2.shape
    tm = TOK_TILE
    bps = seq // tm
    assert seq // MOBA_BLOCK <= MOBA_MAX_BLOCKS
    full = lambda a: pl.BlockSpec(a.shape, lambda i: (0,) * a.ndim)
    per_b = pl.BlockSpec((None, 1, d), lambda i: (i // bps, 0, 0))
    tok = lambda w: pl.BlockSpec((tm, w), lambda i: (i, 0))
    widths = (MLA_HEADS * HEAD_PAD, MLA_HEADS * HEAD_PAD, MLA_HEADS * MLA_VDIM,
              MOBA_HEADS * HEAD_PAD, MOBA_HEADS * HEAD_PAD, MOBA_HEADS * MOBA_HDIM)
    return pl.pallas_call(
        functools.partial(_inproj_kernel, bps),
        out_shape=[jax.ShapeDtypeStruct((n, w), BF16) for w in widths],
        grid=(n // tm,),
        in_specs=[tok(d), per_b, per_b, full(g_mix), full(w_in_p), full(g_qa), full(w_uq_p),
                  full(g_kva), full(w_k_p), full(w_v), tok(1), full(inv_row)],
        out_specs=[tok(w) for w in widths],
        scratch_shapes=[pltpu.VMEM((MOBA_HEADS * MOBA_MAX_BLOCKS, MOBA_HEADS * HEAD_PAD), F32)],
        compiler_params=_cparams(("arbitrary",)),
        name="inproj",
    )(x2, sc, sh, g_mix, w_in_p, g_qa, w_uq_p, g_kva, w_k_p, w_v, pos, inv_row)


def _attn_kernel(moba, q_ref, k_ref, v_ref, *rest):
    if moba:
        t_ref, o_ref = rest
    else:
        (o_ref,) = rest
    bq = pl.program_id(2)
    tq = q_ref.shape[0]
    vd = v_ref.shape[1] // 2
    r_i = lax.broadcasted_iota(jnp.int32, (tq, tq), 0)
    c_i = lax.broadcasted_iota(jnp.int32, (tq, tq), 1)
    causal = r_i >= c_i
    lane_k = lax.broadcasted_iota(jnp.int32, (1, HEAD_PAD), 1)

    def scores(hh, n, diag):
        sl = slice(hh * HEAD_PAD, (hh + 1) * HEAD_PAD)
        k = k_ref[pl.ds(pl.multiple_of(n * tq, tq), tq), sl]
        if moba and not diag:
            k = k + jnp.where(lane_k == MOBA_HDIM + n, 1.0, 0.0).astype(BF16)
        s = _dot_nt(q_ref[:, sl], k)
        if moba:
            s = s + t_ref[hh, bq - n]
        if diag:
            s = jnp.where(causal, s, MASK_NEG)
        return s

    def update(n0, nblocks, carry, diag):
        v = v_ref[pl.ds(pl.multiple_of(n0 * tq, tq), nblocks * tq), :]
        new = []
        for hh in range(2):
            m, l, acc = carry[hh]
            ss = [scores(hh, n0 + j, diag) for j in range(nblocks)]
            m_new = m
            for s in ss:
                m_new = jnp.maximum(m_new, jnp.max(s, axis=-1, keepdims=True))
            alpha = jnp.exp(m - m_new)
            ps = [jnp.exp(s - m_new) for s in ss]
            l = alpha * l
            for p in ps:
                l = l + jnp.sum(p, axis=-1, keepdims=True)
            p_all = jnp.concatenate([p.astype(BF16) for p in ps], axis=1)
            acc = alpha * acc + jnp.dot(p_all, v, preferred_element_type=F32)
            new.append((m_new, l, acc))
        return tuple(new)

    init = (jnp.full((tq, 1), -jnp.inf, F32), jnp.zeros((tq, 1), F32),
            jnp.zeros((tq, v_ref.shape[1]), F32))
    carry = update(bq, 1, (init, init), True)
    carry = lax.fori_loop(0, bq % 2, lambda n, c: update(n, 1, c, False), carry)
    carry = lax.fori_loop(0, bq // 2, lambda i, c: update(bq % 2 + 2 * i, 2, c, False), carry)
    outs = [acc / l for (_, l, acc) in carry]
    lane_o = lax.broadcasted_iota(jnp.int32, outs[0].shape, 1)
    o_ref[...] = jnp.where(lane_o < vd, outs[0], outs[1]).astype(o_ref.dtype)


def _attn(q, k, v, seq, bias_tab=None):
    n = q.shape[0]
    bsz = n // seq
    heads = q.shape[1] // HEAD_PAD
    tq = MOBA_BLOCK
    nq = seq // tq
    moba = bias_tab is not None
    in_specs = [pl.BlockSpec((tq, 2 * HEAD_PAD), lambda b, p, i: (b * nq + i, p)),
                pl.BlockSpec((seq, 2 * HEAD_PAD), lambda b, p, i: (b, p)),
                pl.BlockSpec((seq, HEAD_PAD), lambda b, p, i: (b, p))]
    args = [q, k, v]
    if moba:
        in_specs.append(pl.BlockSpec((2,) + bias_tab.shape[1:], lambda b, p, i: (p, 0, 0, 0)))
        args.append(bias_tab)
    return pl.pallas_call(
        functools.partial(_attn_kernel, moba),
        out_shape=jax.ShapeDtypeStruct((n, v.shape[1]), BF16),
        grid=(bsz, heads // 2, nq),
        in_specs=in_specs,
        out_specs=pl.BlockSpec((tq, HEAD_PAD), lambda b, p, i: (b * nq + i, p)),
        compiler_params=_cparams(("arbitrary", "arbitrary", "arbitrary")),
        name="moba_attn" if moba else "mla_attn",
    )(*args)


def _top16(s, v_ref, r_ref=None):
    r = jnp.full(s.shape, float(PEER_TOPK), F32)
    for a in range(PEER_TOPK):
        m = jnp.max(s, axis=0, keepdims=True)
        hit = s == m
        if r_ref is not None:
            r = jnp.where(hit, float(a), r)
        s = jnp.where(hit, -jnp.inf, s)
        v_ref[a:a + 1, :] = m
    if r_ref is not None:
        _store_lane_tiles(r_ref, r)


def _store_lane_tiles(ref, val):
    for lt in range(val.shape[1] // LANE):
        ref[lt] = val[:, lt * LANE:(lt + 1) * LANE]


def _mid_kernel(x_ref, oa_ref, ob_ref, wo_ref, gta_ref, scf_ref, shf_ref, gffn_ref, wpq_ref,
                kpad_ref, x1_ref, h2_ref, r1_ref, e1_ref, k0_ref, e0_ref, v0_s, v1_s):
    half = oa_ref.shape[1]
    y = (jnp.dot(oa_ref[...], wo_ref[:half, :], preferred_element_type=F32)
         + jnp.dot(ob_ref[...], wo_ref[half:, :], preferred_element_type=F32))
    x1 = x_ref[...] + gta_ref[...] * y
    x1_ref[...] = x1
    h2 = (_rms(x1, gffn_ref[...]) * (1.0 + scf_ref[...]) + shf_ref[...]).astype(BF16)
    h2_ref[...] = h2
    pq = jnp.dot(h2, wpq_ref[...], preferred_element_type=F32).astype(BF16)
    nk = PEER_NKEYS
    for hd in range(PEER_HEADS):
        st = _dot_nt(kpad_ref[hd], pq[:, hd * LANE:(hd + 1) * LANE])
        s0 = st[:nk, :]
        s1 = st[nk:, :]
        _top16(s0, v0_s)
        _top16(s1, v1_s, r1_ref.at[hd])
        v0 = v0_s[...]
        v1 = v1_s[...]
        groups = [v0[0:1, :] + v1[0:8, :], v0[0:1, :] + v1[8:16, :]]
        groups += [v0[a:a + 1, :] + v1[0:8, :] for a in range(1, 8)]
        groups.append(v0[8:16, :] + v1[0:1, :])
        top = groups[0][0:1, :]
        work = list(groups)
        thr = top
        for _ in range(PEER_TOPK):
            thr = functools.reduce(jnp.maximum, [jnp.max(g, axis=0, keepdims=True) for g in work])
            work = [jnp.where(g == thr, -jnp.inf, g) for g in work]
        z = functools.reduce(
            jnp.add, [jnp.sum(jnp.where(g >= thr, jnp.exp(g - top), 0.0), axis=0, keepdims=True)
                      for g in groups])
        cnt = jnp.zeros(s0.shape, F32)
        for b in range(PEER_TOPK):
            cnt = cnt + jnp.where(s0 + v1[b:b + 1, :] >= thr, 1.0, 0.0)
        _store_lane_tiles(k0_ref.at[hd], cnt)
        _store_lane_tiles(e0_ref.at[hd], jnp.exp(s0 - v0[0:1, :]) / z)
        _store_lane_tiles(e1_ref.at[hd], jnp.exp(s1 - v1[0:1, :]))


def _mid(x2, o_mla, o_moba, w_out, gta, scf, shf, g_ffn, w_pq, kpad, seq):
    n, d = x2.shape
    tm = TOK_TILE
    bps = seq // tm
    full = lambda a: pl.BlockSpec(a.shape, lambda i: (0,) * a.ndim)
    per_b = pl.BlockSpec((None, 1, d), lambda i: (i // bps, 0, 0))
    tok = lambda w: pl.BlockSpec((tm, w), lambda i: (i, 0))
    sel_shape = jax.ShapeDtypeStruct((PEER_HEADS, n // LANE, PEER_NKEYS, LANE), F32)
    sel_spec = pl.BlockSpec((PEER_HEADS, tm // LANE, PEER_NKEYS, LANE), lambda i: (0, i, 0, 0))
    return pl.pallas_call(
        _mid_kernel,
        out_shape=[jax.ShapeDtypeStruct((n, d), F32), jax.ShapeDtypeStruct((n, d), BF16),
                   sel_shape, sel_shape, sel_shape, sel_shape],
        grid=(n // tm,),
        in_specs=[tok(d), tok(o_mla.shape[1]), tok(o_moba.shape[1]), full(w_out), per_b, per_b,
                  per_b, full(g_ffn), full(w_pq), full(kpad)],
        out_specs=[tok(d), tok(d), sel_spec, sel_spec, sel_spec, sel_spec],
        scratch_shapes=[pltpu.VMEM((PEER_TOPK, tm), F32), pltpu.VMEM((PEER_TOPK, tm), F32)],
        compiler_params=_cparams(("arbitrary",)),
        name="mid",
    )(x2, o_mla, o_moba, w_out, gta, scf, shf, g_ffn, w_pq, kpad)


def _peer_kernel(h_ref, u_ref, vt_ref, r1_ref, e1_ref, k0_ref, e0_ref, x1_ref, gtf_ref, gfin_ref,
                 o_ref, acc_ref, at_ref, p_ref):
    e = pl.program_id(1)
    ltiles, te, _ = at_ref.shape

    @pl.when(e == 0)
    def _():
        acc_ref[...] = jnp.zeros_like(acc_ref)

    def expert_scores(sb):
        slab = slice(sb * PEER_SLAB, (sb + 1) * PEER_SLAB)
        at = _dot_nt(u_ref[slab, :], h_ref[...])
        for lt in range(ltiles):
            at_ref[lt, slab, :] = at[:, lt * LANE:(lt + 1) * LANE]

    nslab = te // PEER_SLAB
    expert_scores(0)
    for sb in range(nslab):
        slab = slice(sb * PEER_SLAB, (sb + 1) * PEER_SLAB)
        if sb + 1 < nslab:
            expert_scores(sb + 1)
        for ig in range(sb * PEER_SLAB // PEER_NKEYS, (sb + 1) * PEER_SLAB // PEER_NKEYS):
            for jc in range(PEER_NKEYS // PEER_JCHUNK):
                jrows = slice(jc * PEER_JCHUNK, (jc + 1) * PEER_JCHUNK)
                rows = slice(ig * PEER_NKEYS + jc * PEER_JCHUNK,
                             ig * PEER_NKEYS + (jc + 1) * PEER_JCHUNK)
                for lt in range(ltiles):
                    w = jnp.zeros((PEER_JCHUNK, LANE), F32)
                    for hd in range(PEER_HEADS):
                        k0 = k0_ref[hd, lt, ig:ig + 1, :]
                        e0 = e0_ref[hd, lt, ig:ig + 1, :]
                        w = w + jnp.where(r1_ref[hd, lt, jrows, :] < k0,
                                          e1_ref[hd, lt, jrows, :], 0.0) * e0
                    a = at_ref[lt, rows, :]
                    gelu = 0.5 * a * (1.0 + lax.erf(a * (1.0 / math.sqrt(2.0))))
                    p_ref[rows, lt * LANE:(lt + 1) * LANE] = (gelu * w).astype(BF16)
        acc_ref[...] += jnp.dot(vt_ref[:, slab], p_ref[slab, :], preferred_element_type=F32)

    @pl.when(e == pl.num_programs(1) - 1)
    def _():
        x2 = x1_ref[...] + gtf_ref[...] * acc_ref[...].T
        o_ref[...] = _rms(x2, gfin_ref[...])


def _peer(h2, u_b, vt_b, r1, e1, k0, e0, x1, gtf, g_fin, seq):
    n, d = h2.shape
    ne = u_b.shape[0]
    tm, te = PEER_TM, PEER_TE
    bps = seq // tm
    sel_spec = pl.BlockSpec((PEER_HEADS, tm // LANE, PEER_NKEYS, LANE), lambda t, e: (0, t, 0, 0))
    key_spec = pl.BlockSpec((PEER_HEADS, tm // LANE, te // PEER_NKEYS, LANE),
                            lambda t, e: (0, t, e, 0))
    return pl.pallas_call(
        _peer_kernel,
        out_shape=jax.ShapeDtypeStruct((n, d), F32),
        grid=(n // tm, ne // te),
        in_specs=[pl.BlockSpec((tm, d), lambda t, e: (t, 0)),
                  pl.BlockSpec((te, d), lambda t, e: (e, 0)),
                  pl.BlockSpec((d, te), lambda t, e: (0, e)),
                  sel_spec, sel_spec, key_spec, key_spec,
                  pl.BlockSpec((tm, d), lambda t, e: (t, 0)),
                  pl.BlockSpec((None, 1, d), lambda t, e: (t // bps, 0, 0)),
                  pl.BlockSpec((1, d), lambda t, e: (0, 0))],
        out_specs=pl.BlockSpec((tm, d), lambda t, e: (t, 0)),
        scratch_shapes=[pltpu.VMEM((d, tm), F32), pltpu.VMEM((tm // LANE, te, LANE), F32),
                        pltpu.VMEM((te, tm), BF16)],
        compiler_params=_cparams(("arbitrary", "arbitrary")),
        name="peer",
    )(h2, u_b, vt_b, r1, e1, k0, e0, x1, gtf, g_fin)


def _pad_heads(w, heads, width):
    k = w.shape[0]
    w = w.reshape(k, heads, width)
    return jnp.pad(w, ((0, 0), (0, 0), (0, HEAD_PAD - width))).reshape(k, heads * HEAD_PAD)


def _rel_bias_table(rel_bias, nblk):
    bs = MOBA_BLOCK
    dist = jnp.arange(-bs, nblk * bs, dtype=jnp.int32)
    nn = jnp.maximum(dist, 0)
    max_exact = REL_BUCKETS // 2
    nf = jnp.maximum(nn, max_exact).astype(F32)
    large = max_exact + (jnp.log(nf / max_exact) / math.log(REL_MAX_DIST / max_exact)
                         * (REL_BUCKETS - max_exact)).astype(jnp.int32)
    large = jnp.minimum(large, REL_BUCKETS - 1)
    bucket = jnp.where(nn < max_exact, nn, large)
    f = rel_bias.astype(F32)[bucket].T
    heads = f.shape[0]
    idx = (bs * jnp.arange(nblk)[:, None] + (bs - 1) - jnp.arange(2 * bs)[None, :]) + bs
    w = f[:, idx]
    rows = jnp.tile(w, (1, 1, bs))[:, :, :bs * (2 * bs - 1)].reshape(heads, nblk, bs, 2 * bs - 1)
    return rows[:, :, :, bs - 1:]


def kernel(x, c, positions, w_ada, b_ada, g_mix, w_in, g_qa, w_uq, g_kva, w_ukv, rel_bias,
           w_out, g_ffn, w_pq, peer_keys, peer_u, peer_v, g_final):
    bsz, seq, d = x.shape
    n = bsz * seq
    nblk = seq // MOBA_BLOCK
    x2 = x.reshape(n, d)
    assert w_ada.shape[0] == 1, "single-layer block: the final norm is fused into the peer kernel"
    for l in range(1):
        mod = _ada(c, w_ada[l], b_ada[l]).reshape(bsz, N_MOD, 1, d)
        sh_a, sc_a, gt_a, sh_f, sc_f, gt_f = (mod[:, j] for j in range(N_MOD))

        cq, ckv, kr, mq, mk, mv = jnp.split(w_in[l], np.cumsum(
            [MLA_Q_LORA, MLA_KV_LORA, MLA_ROPE, MOBA_HEADS * MOBA_HDIM, MOBA_HEADS * MOBA_HDIM])
            .tolist(), axis=1)
        kr_p = jnp.pad(kr, ((0, 0), (ROPE_LO, HEAD_PAD - ROPE_LO - MLA_ROPE)))
        w_in_p = jnp.concatenate(
            [cq, ckv, kr_p, _pad_heads(mq, MOBA_HEADS, MOBA_HDIM),
             _pad_heads(mk, MOBA_HEADS, MOBA_HDIM), mv], axis=1).astype(BF16)
        w_uq_p = _pad_heads(w_uq[l], MLA_HEADS, MLA_NOPE + MLA_ROPE).astype(BF16)
        w_kv = w_ukv[l].reshape(MLA_KV_LORA, MLA_HEADS, MLA_NOPE + MLA_VDIM)
        w_k_p = _pad_heads(w_kv[:, :, :MLA_NOPE].reshape(MLA_KV_LORA, -1), MLA_HEADS,
                           MLA_NOPE).astype(BF16)
        w_v = w_kv[:, :, MLA_NOPE:].reshape(MLA_KV_LORA, -1).astype(BF16)
        inv = ROPE_THETA ** (-jnp.arange(ROPE_HALF, dtype=F32) / ROPE_HALF)
        inv_row = jnp.zeros((1, LANE), F32).at[0, ROPE_LO:ROPE_LO + MLA_ROPE].set(
            jnp.concatenate([inv, inv]))
        pos = positions.reshape(n, 1)

        q_mla, k_mla, v_mla, q_mb, k_mb, v_mb = _inproj(
            x2, sc_a, sh_a, g_mix[l].reshape(1, d), w_in_p, g_qa[l].reshape(1, -1), w_uq_p,
            g_kva[l].reshape(1, -1), w_k_p, w_v, pos, inv_row, seq)

        o_mla = _attn(q_mla, k_mla, v_mla, seq)
        o_moba = _attn(q_mb, k_mb, v_mb, seq, _rel_bias_table(rel_bias, nblk))

        keys = peer_keys[l]
        zer = jnp.zeros_like(keys[:, 0])
        kpad = jnp.concatenate([jnp.concatenate([keys[:, 0], zer], axis=2),
                                jnp.concatenate([zer, keys[:, 1]], axis=2)], axis=1).astype(BF16)
        x1, h2, r1, e1, k0, e0 = _mid(
            x2, o_mla, o_moba, w_out[l].astype(BF16), gt_a, sc_f, sh_f, g_ffn[l].reshape(1, d),
            w_pq[l].astype(BF16), kpad, seq)

        x2 = _peer(h2, peer_u[l].astype(BF16), peer_v[l].T.astype(BF16), r1, e1, k0, e0, x1,
                   gt_f, g_final.reshape(1, d), seq)
    return x2.reshape(bsz, seq, d)
```

```python
import functools
import math

import jax
import jax.numpy as jnp
import numpy as np
from jax import lax
from jax.experimental import pallas as pl
from jax.experimental.pallas import tpu as pltpu

F32 = jnp.float32
BF16 = jnp.bfloat16
HIGHEST = lax.Precision.HIGHEST

D_MODEL = 1024
N_MOD = 6
NORM_EPS = 1e-6
MLA_HEADS = 8
MLA_Q_LORA = 768
MLA_KV_LORA = 256
MLA_NOPE = 64
MLA_ROPE = 32
MLA_VDIM = 64
ROPE_THETA = 10000.0
MOBA_HEADS = 8
MOBA_HDIM = 64
MOBA_BLOCK = 256
MOBA_TOPK = 3
REL_BUCKETS = 32
REL_MAX_DIST = 1024
PEER_NKEYS = 128
PEER_HEADS = 8
PEER_TOPK = 16
PEER_HALF = 64

LANE = 128
HEAD_PAD = 128
ROPE_LO = MLA_NOPE
ROPE_HALF = MLA_ROPE // 2
MASK_NEG = -1e9
VMEM_LIMIT = 56 * 1024 * 1024
TOK_TILE = MOBA_BLOCK
MOBA_MAX_BLOCKS = LANE // MOBA_HEADS
ATTN_HEADS = 4
PEER_TM = 512
PEER_TE = 2048
PEER_SLAB = 256
BF16_ROWS = 16


def _cparams(sem, flags=None):
    return pltpu.CompilerParams(dimension_semantics=sem, vmem_limit_bytes=VMEM_LIMIT, flags=flags)


def _rms(x, g):
    ms = jnp.mean(x * x, axis=-1, keepdims=True)
    return x * lax.rsqrt(ms + NORM_EPS) * g


def _dot_nt(a, b, precision=None):
    return lax.dot_general(a, b, (((1,), (1,)), ((), ())), precision=precision,
                           preferred_element_type=F32)


def _ada_kernel(c_ref, w_ref, b_ref, o_ref):
    c = c_ref[...]
    s = c / (1.0 + jnp.exp(-c))
    o_ref[...] = jnp.dot(s, w_ref[...], precision=HIGHEST, preferred_element_type=F32) + b_ref[...]


def _ada(c, w, b):
    bsz, d = c.shape
    n = w.shape[1]
    return pl.pallas_call(
        _ada_kernel,
        out_shape=jax.ShapeDtypeStruct((bsz, n), F32),
        grid=(n // d,),
        in_specs=[pl.BlockSpec((bsz, d), lambda j: (0, 0)),
                  pl.BlockSpec((d, d), lambda j: (0, j)),
                  pl.BlockSpec((1, d), lambda j: (0, j))],
        out_specs=pl.BlockSpec((bsz, d), lambda j: (0, j)),
        compiler_params=_cparams(("arbitrary",)),
        name="ada",
    )(c, w, b.reshape(1, n))


def _rope_block(x, cosm, sinm, lane):
    up = pltpu.roll(x, LANE - ROPE_HALF, 1)
    dn = pltpu.roll(x, ROPE_HALF, 1)
    rot = jnp.where(lane < ROPE_LO + ROPE_HALF, -up, dn)
    return x * cosm + rot * sinm


def _inproj_kernel(blocks_per_seq,
                   x_ref, sc_ref, sh_ref, gmix_ref, win_ref, gqa_ref, wuq_ref, gkva_ref,
                   wk_ref, wv_ref, pos_ref, inv_ref,
                   qmla_ref, kmla_ref, vmla_ref, mq_ref, mk_ref, mv_ref,
                   kmt_ref):
    i = pl.program_id(0)
    own = i % blocks_per_seq
    tm = x_ref.shape[0]

    @pl.when(i == 0)
    def _():
        kmt_ref[...] = jnp.zeros_like(kmt_ref)

    h = _rms(x_ref[...], gmix_ref[...]) * (1.0 + sc_ref[...]) + sh_ref[...]
    z = jnp.dot(h.astype(BF16), win_ref[...], preferred_element_type=F32)
    o = 0
    c_q = z[:, o:o + MLA_Q_LORA]; o += MLA_Q_LORA
    c_kv = z[:, o:o + MLA_KV_LORA]; o += MLA_KV_LORA
    k_r = z[:, o:o + HEAD_PAD]; o += HEAD_PAD
    mq = z[:, o:o + MOBA_HEADS * HEAD_PAD]; o += MOBA_HEADS * HEAD_PAD
    mk = z[:, o:o + MOBA_HEADS * HEAD_PAD]; o += MOBA_HEADS * HEAD_PAD
    mv = z[:, o:o + MOBA_HEADS * MOBA_HDIM]

    lane = lax.broadcasted_iota(jnp.int32, (tm, LANE), 1)
    in_rope = (lane >= ROPE_LO) & (lane < ROPE_LO + MLA_ROPE)
    ang = pos_ref[...].astype(F32) * inv_ref[...]
    cosm = jnp.where(in_rope, jnp.cos(ang), 1.0)
    sinm = jnp.where(in_rope, jnp.sin(ang), 0.0)

    qn = _rms(c_q, gqa_ref[...]).astype(BF16)
    q = jnp.dot(qn, wuq_ref[...], preferred_element_type=F32)
    q_scale = (MLA_NOPE + MLA_ROPE) ** -0.5
    kvn = _rms(c_kv, gkva_ref[...]).astype(BF16)
    kn = jnp.dot(kvn, wk_ref[...], preferred_element_type=F32)
    k_rr = _rope_block(k_r, cosm, sinm, lane)
    for hd in range(MLA_HEADS):
        sl = slice(hd * HEAD_PAD, (hd + 1) * HEAD_PAD)
        qmla_ref[:, sl] = (_rope_block(q[:, sl], cosm, sinm, lane) * q_scale).astype(BF16)
        kmla_ref[:, sl] = (kn[:, sl] + k_rr).astype(BF16)
    vmla_ref[...] = jnp.dot(kvn, wv_ref[...], preferred_element_type=F32).astype(BF16)

    gate_t = _dot_nt(kmt_ref[...], mq, precision=HIGHEST)
    nblk = kmt_ref.shape[0] // MOBA_HEADS
    row = lax.broadcasted_iota(jnp.int32, (nblk, tm), 0).astype(F32)
    valid = row < own.astype(F32)
    bias_rows = []
    for hd in range(MOBA_HEADS):
        g = jnp.where(valid, gate_t[hd * nblk:(hd + 1) * nblk, :], -jnp.inf)
        bias = jnp.full((nblk, tm), MASK_NEG, F32)
        for _ in range(MOBA_TOPK):
            m = jnp.max(g, axis=0, keepdims=True)
            first = jnp.min(jnp.where(g == m, row, float(nblk)), axis=0, keepdims=True)
            pick = (row == first) & (m > -jnp.inf)
            bias = jnp.where(pick, 0.0, bias)
            g = jnp.where(pick, -jnp.inf, g)
        bias_rows.append(bias)
    sel_tm = jnp.concatenate(bias_rows, axis=0).T
    in_sel = (lane >= MOBA_HDIM) & (lane < MOBA_HDIM + nblk)
    m_scale = MOBA_HDIM ** -0.5
    for hd in range(MOBA_HEADS):
        sl = slice(hd * HEAD_PAD, (hd + 1) * HEAD_PAD)
        placed = pltpu.roll(sel_tm, (MOBA_HDIM - nblk * hd) % LANE, 1)
        mq_ref[:, sl] = jnp.where(in_sel, placed, mq[:, sl] * m_scale).astype(BF16)
    mk_ref[...] = mk.astype(BF16)
    mv_ref[...] = mv.astype(BF16)

    kmean = jnp.mean(mk, axis=0, keepdims=True)
    lane_k = lax.broadcasted_iota(jnp.int32, kmean.shape, 1)
    for hd in range(MOBA_HEADS):
        in_head = (lane_k >= hd * HEAD_PAD) & (lane_k < (hd + 1) * HEAD_PAD)
        kmt_ref[pl.ds(hd * nblk + own, 1), :] = jnp.where(in_head, kmean, 0.0)


def _inproj(x2, sc, sh, g_mix, w_in_p, g_qa, w_uq_p, g_kva, w_k_p, w_v, pos, inv_row, seq):
    n, d = x2.shape
    tm = TOK_TILE
    bps = seq // tm
    assert seq // MOBA_BLOCK <= MOBA_MAX_BLOCKS
    full = lambda a: pl.BlockSpec(a.shape, lambda i: (0,) * a.ndim)
    per_b = pl.BlockSpec((None, 1, d), lambda i: (i // bps, 0, 0))
    tok = lambda w: pl.BlockSpec((tm, w), lambda i: (i, 0))
    widths = (MLA_HEADS * HEAD_PAD, MLA_HEADS * HEAD_PAD, MLA_HEADS * MLA_VDIM,
              MOBA_HEADS * HEAD_PAD, MOBA_HEADS * HEAD_PAD, MOBA_HEADS * MOBA_HDIM)
    return pl.pallas_call(
        functools.partial(_inproj_kernel, bps),
        out_shape=[jax.ShapeDtypeStruct((n, w), BF16) for w in widths],
        grid=(n // tm,),
        in_specs=[tok(d), per_b, per_b, full(g_mix), full(w_in_p), full(g_qa), full(w_uq_p),
                  full(g_kva), full(w_k_p), full(w_v), tok(1), full(inv_row)],
        out_specs=[tok(w) for w in widths],
        scratch_shapes=[pltpu.VMEM((MOBA_HEADS * MOBA_MAX_BLOCKS, MOBA_HEADS * HEAD_PAD), F32)],
        compiler_params=_cparams(("arbitrary",)),
        name="inproj",
    )(x2, sc, sh, g_mix, w_in_p, g_qa, w_uq_p, g_kva, w_k_p, w_v, pos, inv_row)


def _attn_kernel(moba, q_ref, k_ref, v_ref, *rest):
    if moba:
        t_ref, o_ref = rest
    else:
        (o_ref,) = rest
    bq = pl.program_id(2)
    tq = q_ref.shape[0]
    nheads = q_ref.shape[1] // HEAD_PAD
    vd = HEAD_PAD // 2
    r_i = lax.broadcasted_iota(jnp.int32, (tq, tq), 0)
    c_i = lax.broadcasted_iota(jnp.int32, (tq, tq), 1)
    causal = r_i >= c_i
    lane_k = lax.broadcasted_iota(jnp.int32, (1, HEAD_PAD), 1)

    def scores(hh, n, diag):
        sl = slice(hh * HEAD_PAD, (hh + 1) * HEAD_PAD)
        k = k_ref[pl.ds(pl.multiple_of(n * tq, tq), tq), sl]
        if moba and not diag:
            k = k + jnp.where(lane_k == MOBA_HDIM + n, 1.0, 0.0).astype(BF16)
        s = _dot_nt(q_ref[:, sl], k)
        if moba:
            s = s + t_ref[hh, bq - n]
        if diag:
            s = jnp.where(causal, s, MASK_NEG)
        return s

    def update(n0, nblocks, carry, diag):
        krows = pl.ds(pl.multiple_of(n0 * tq, tq), nblocks * tq)
        new = []
        for hh in range(nheads):
            m, l, acc = carry[hh]
            v = v_ref[krows, (hh // 2) * HEAD_PAD:(hh // 2 + 1) * HEAD_PAD]
            ss = [scores(hh, n0 + j, diag) for j in range(nblocks)]
            m_new = m
            for s in ss:
                m_new = jnp.maximum(m_new, jnp.max(s, axis=-1, keepdims=True))
            alpha = jnp.exp(m - m_new)
            ps = [jnp.exp(s - m_new) for s in ss]
            l = alpha * l
            for p in ps:
                l = l + jnp.sum(p, axis=-1, keepdims=True)
            p_all = jnp.concatenate([p.astype(BF16) for p in ps], axis=1)
            acc = alpha * acc + jnp.dot(p_all, v, preferred_element_type=F32)
            new.append((m_new, l, acc))
        return tuple(new)

    init = (jnp.full((tq, 1), -jnp.inf, F32), jnp.zeros((tq, 1), F32),
            jnp.zeros((tq, HEAD_PAD), F32))
    carry = update(bq, 1, (init,) * nheads, True)
    carry = lax.fori_loop(0, bq % 2, lambda n, c: update(n, 1, c, False), carry)
    carry = lax.fori_loop(0, bq // 2, lambda i, c: update(bq % 2 + 2 * i, 2, c, False), carry)
    outs = [acc / l for (_, l, acc) in carry]
    lane_o = lax.broadcasted_iota(jnp.int32, outs[0].shape, 1)
    for pr in range(nheads // 2):
        o_ref[:, pr * HEAD_PAD:(pr + 1) * HEAD_PAD] = jnp.where(
            lane_o < vd, outs[2 * pr], outs[2 * pr + 1]).astype(o_ref.dtype)


def _attn(q, k, v, seq, bias_tab=None):
    n = q.shape[0]
    bsz = n // seq
    heads = q.shape[1] // HEAD_PAD
    hs = ATTN_HEADS
    tq = MOBA_BLOCK
    nq = seq // tq
    moba = bias_tab is not None
    in_specs = [pl.BlockSpec((tq, hs * HEAD_PAD), lambda g, b, i: (b * nq + i, g)),
                pl.BlockSpec((seq, hs * HEAD_PAD), lambda g, b, i: (b, g)),
                pl.BlockSpec((seq, hs * HEAD_PAD // 2), lambda g, b, i: (b, g))]
    args = [q, k, v]
    if moba:
        in_specs.append(pl.BlockSpec((hs,) + bias_tab.shape[1:], lambda g, b, i: (g, 0, 0, 0),
                                     pipeline_mode=pl.Buffered(1)))
        args.append(bias_tab)
    return pl.pallas_call(
        functools.partial(_attn_kernel, moba),
        out_shape=jax.ShapeDtypeStruct((n, v.shape[1]), BF16),
        grid=(heads // hs, bsz, nq),
        in_specs=in_specs,
        out_specs=pl.BlockSpec((tq, hs * HEAD_PAD // 2), lambda g, b, i: (b * nq + i, g)),
        compiler_params=_cparams(("arbitrary", "arbitrary", "arbitrary")),
        name="moba_attn" if moba else "mla_attn",
    )(*args)


def _top16(s, v_ref, r_ref=None):
    r = jnp.full(s.shape, float(PEER_TOPK), F32)
    for a in range(PEER_TOPK):
        m = jnp.max(s, axis=0, keepdims=True)
        hit = s == m
        if r_ref is not None:
            r = jnp.where(hit, float(a), r)
        s = jnp.where(hit, -jnp.inf, s)
        v_ref[a:a + 1, :] = m
    if r_ref is not None:
        _store_lane_tiles(r_ref, r)


def _store_lane_tiles(ref, val):
    for lt in range(val.shape[1] // LANE):
        tile = val[:, lt * LANE:(lt + 1) * LANE]
        if ref.dtype == jnp.uint32:
            tile = pltpu.bitcast(tile.astype(BF16), jnp.uint32)
        ref[lt] = tile


def _mid_kernel(x_ref, oa_ref, ob_ref, wo_ref, gta_ref, scf_ref, shf_ref, gffn_ref, wpq_ref,
                kpad_ref, x1_ref, h2_ref, r1_ref, e1_ref, k0_ref, e0_ref, v0_s, v1_s):
    half = oa_ref.shape[1]
    y = (jnp.dot(oa_ref[...], wo_ref[:half, :], preferred_element_type=F32)
         + jnp.dot(ob_ref[...], wo_ref[half:, :], preferred_element_type=F32))
    x1 = x_ref[...] + gta_ref[...] * y
    x1_ref[...] = x1
    h2 = (_rms(x1, gffn_ref[...]) * (1.0 + scf_ref[...]) + shf_ref[...]).astype(BF16)
    h2_ref[...] = h2
    pq = jnp.dot(h2, wpq_ref[...], preferred_element_type=F32).astype(BF16)
    nk = PEER_NKEYS
    for hd in range(PEER_HEADS):
        st = _dot_nt(kpad_ref[hd], pq[:, hd * LANE:(hd + 1) * LANE])
        s0 = st[:nk, :]
        s1 = st[nk:, :]
        _top16(s0, v0_s)
        _top16(s1, v1_s, r1_ref.at[hd])
        v0 = v0_s[...]
        v1 = v1_s[...]
        groups = [v0[0:1, :] + v1[0:8, :], v0[0:1, :] + v1[8:16, :]]
        groups += [v0[a:a + 1, :] + v1[0:8, :] for a in range(1, 8)]
        groups.append(v0[8:16, :] + v1[0:1, :])
        top = groups[0][0:1, :]
        work = list(groups)
        thr = top
        for _ in range(PEER_TOPK):
            thr = functools.reduce(jnp.maximum, [jnp.max(g, axis=0, keepdims=True) for g in work])
            work = [jnp.where(g == thr, -jnp.inf, g) for g in work]
        z = functools.reduce(
            jnp.add, [jnp.sum(jnp.where(g >= thr, jnp.exp(g - top), 0.0), axis=0, keepdims=True)
                      for g in groups])
        cnt = jnp.zeros(s0.shape, F32)
        for b in range(PEER_TOPK):
            cnt = cnt + jnp.where(s0 + v1[b:b + 1, :] >= thr, 1.0, 0.0)
        _store_lane_tiles(k0_ref.at[hd], cnt)
        _store_lane_tiles(e0_ref.at[hd], jnp.exp(s0 - v0[0:1, :]) / z)
        _store_lane_tiles(e1_ref.at[hd], jnp.exp(s1 - v1[0:1, :]))


def _mid(x2, o_mla, o_moba, w_out, gta, scf, shf, g_ffn, w_pq, kpad, seq):
    n, d = x2.shape
    tm = TOK_TILE
    bps = seq // tm
    full = lambda a: pl.BlockSpec(a.shape, lambda i: (0,) * a.ndim)
    per_b = pl.BlockSpec((None, 1, d), lambda i: (i // bps, 0, 0))
    tok = lambda w: pl.BlockSpec((tm, w), lambda i: (i, 0))
    sel_shape = jax.ShapeDtypeStruct((PEER_HEADS, n // LANE, PEER_NKEYS, LANE), F32)
    sel_spec = pl.BlockSpec((PEER_HEADS, tm // LANE, PEER_NKEYS, LANE), lambda i: (0, i, 0, 0))
    pk_shape = jax.ShapeDtypeStruct((PEER_HEADS, n // LANE, PEER_NKEYS // 2, LANE), jnp.uint32)
    pk_spec = pl.BlockSpec((PEER_HEADS, tm // LANE, PEER_NKEYS // 2, LANE), lambda i: (0, i, 0, 0))
    return pl.pallas_call(
        _mid_kernel,
        out_shape=[jax.ShapeDtypeStruct((n, d), F32), jax.ShapeDtypeStruct((n, d), BF16),
                   pk_shape, pk_shape, sel_shape, sel_shape],
        grid=(n // tm,),
        in_specs=[tok(d), tok(o_mla.shape[1]), tok(o_moba.shape[1]), full(w_out), per_b, per_b,
                  per_b, full(g_ffn), full(w_pq), full(kpad)],
        out_specs=[tok(d), tok(d), pk_spec, pk_spec, sel_spec, sel_spec],
        scratch_shapes=[pltpu.VMEM((PEER_TOPK, tm), F32), pltpu.VMEM((PEER_TOPK, tm), F32)],
        compiler_params=_cparams(("arbitrary",)),
        name="mid",
    )(x2, o_mla, o_moba, w_out, gta, scf, shf, g_ffn, w_pq, kpad)


def _peer_kernel(h_ref, u_ref, vt_ref, r1_ref, e1_ref, k0_ref, e0_ref, x1_ref, gtf_ref, gfin_ref,
                 o_ref, acc_ref, ht_ref, at0_ref, at1_ref, p0_ref, p1_ref, kb_ref, eb_ref):
    e = pl.program_id(1)
    ltiles = at0_ref.shape[0]
    te = u_ref.shape[0]
    ngroups = te // PEER_NKEYS
    at_bufs = (at0_ref, at1_ref)
    p_bufs = (p0_ref, p1_ref)

    @pl.when(e == 0)
    def _():
        acc_ref[...] = jnp.zeros_like(acc_ref)
        ht_ref[...] = h_ref[...].T

    for hd in range(PEER_HEADS):
        for lt in range(ltiles):
            for ig in range(ngroups):
                idx = (hd * ltiles + lt) * ngroups + ig
                kb_ref[idx] = jnp.broadcast_to(
                    k0_ref[hd, lt, ig:ig + 1, :], (BF16_ROWS, LANE)).astype(BF16)
                eb_ref[idx] = jnp.broadcast_to(
                    e0_ref[hd, lt, ig:ig + 1, :], (BF16_ROWS, LANE)).astype(BF16)

    def expert_scores(sb, par):
        slab = pl.ds(pl.multiple_of(sb * PEER_SLAB, PEER_SLAB), PEER_SLAB)
        at = jnp.dot(u_ref[slab, :], ht_ref[...], preferred_element_type=F32)
        for lt in range(ltiles):
            at_bufs[par][lt] = at[:, lt * LANE:(lt + 1) * LANE]

    def expert_weights(sb, par):
        for g in range(PEER_SLAB // PEER_NKEYS):
            ig = sb * (PEER_SLAB // PEER_NKEYS) + g
            for lt in range(ltiles):
                k0 = [kb_ref[(hd * ltiles + lt) * ngroups + ig] for hd in range(PEER_HEADS)]
                e0 = [eb_ref[(hd * ltiles + lt) * ngroups + ig] for hd in range(PEER_HEADS)]
                for jb in range(PEER_NKEYS // BF16_ROWS):
                    wrows = slice(jb * BF16_ROWS // 2, (jb + 1) * BF16_ROWS // 2)
                    rows = slice(g * PEER_NKEYS + jb * BF16_ROWS,
                                 g * PEER_NKEYS + (jb + 1) * BF16_ROWS)
                    w = None
                    for hd in range(PEER_HEADS):
                        e1 = pltpu.bitcast(e1_ref[hd, lt, wrows, :], BF16)
                        r1 = pltpu.bitcast(r1_ref[hd, lt, wrows, :], BF16)
                        term = jnp.where(r1 < k0[hd], e1, jnp.zeros_like(e1)) * e0[hd]
                        w = term if w is None else w + term
                    a = at_bufs[par][lt, rows, :]
                    gelu = 0.5 * a * (1.0 + lax.erf(a * (1.0 / math.sqrt(2.0))))
                    p_bufs[par][rows, lt * LANE:(lt + 1) * LANE] = gelu.astype(BF16) * w

    def expert_out(sb, par):
        acc_ref[...] += jnp.dot(vt_ref[sb], p_bufs[par][...], preferred_element_type=F32)

    nslab = te // PEER_SLAB
    assert nslab % 2 == 0 and nslab >= 4

    def slab_pair(j, first=False, last=False):
        sb = 2 * j
        expert_scores(sb + 1, 1)
        expert_weights(sb, 0)
        if not first:
            expert_out(sb - 1, 1)
        if not last:
            expert_scores(sb + 2, 0)
        expert_weights(sb + 1, 1)
        expert_out(sb, 0)

    expert_scores(0, 0)
    slab_pair(0, first=True)

    def pipelined(j, carry):
        slab_pair(j)
        return carry

    lax.fori_loop(1, nslab // 2 - 1, pipelined, 0)
    slab_pair(nslab // 2 - 1, last=True)
    expert_out(nslab - 1, 1)

    @pl.when(e == pl.num_programs(1) - 1)
    def _():
        x2 = x1_ref[...] + gtf_ref[...] * acc_ref[...].T
        o_ref[...] = _rms(x2, gfin_ref[...])


def _peer(h2, u_b, vt_b, r1, e1, k0, e0, x1, gtf, g_fin, seq):
    n, d = h2.shape
    ne = u_b.shape[0]
    tm, te = PEER_TM, PEER_TE
    bps = seq // tm
    sel_spec = pl.BlockSpec((PEER_HEADS, tm // LANE, PEER_NKEYS // 2, LANE),
                            lambda t, e: (0, t, 0, 0))
    key_spec = pl.BlockSpec((PEER_HEADS, tm // LANE, te // PEER_NKEYS, LANE),
                            lambda t, e: (0, t, e, 0))
    return pl.pallas_call(
        _peer_kernel,
        out_shape=jax.ShapeDtypeStruct((n, d), F32),
        grid=(n // tm, ne // te),
        in_specs=[pl.BlockSpec((tm, d), lambda t, e: (t, 0)),
                  pl.BlockSpec((te, d), lambda t, e: (e, 0)),
                  pl.BlockSpec((te // PEER_SLAB, d, PEER_SLAB), lambda t, e: (e, 0, 0)),
                  sel_spec, sel_spec, key_spec, key_spec,
                  pl.BlockSpec((tm, d), lambda t, e: (t, 0)),
                  pl.BlockSpec((None, 1, d), lambda t, e: (t // bps, 0, 0)),
                  pl.BlockSpec((1, d), lambda t, e: (0, 0))],
        out_specs=pl.BlockSpec((tm, d), lambda t, e: (t, 0)),
        scratch_shapes=[pltpu.VMEM((d, tm), F32), pltpu.VMEM((d, tm), BF16)]
        + [pltpu.VMEM((tm // LANE, PEER_SLAB, LANE), F32)] * 2
        + [pltpu.VMEM((PEER_SLAB, tm), BF16)] * 2
        + [pltpu.VMEM((PEER_HEADS * (tm // LANE) * (te // PEER_NKEYS), BF16_ROWS, LANE), BF16)] * 2,
        compiler_params=_cparams(("arbitrary", "arbitrary")),
        name="peer",
    )(h2, u_b, vt_b, r1, e1, k0, e0, x1, gtf, g_fin)


def _pad_heads(w, heads, width):
    k = w.shape[0]
    w = w.reshape(k, heads, width)
    return jnp.pad(w, ((0, 0), (0, 0), (0, HEAD_PAD - width))).reshape(k, heads * HEAD_PAD)


def _rel_bias_table(rel_bias, nblk):
    bs = MOBA_BLOCK
    dist = jnp.arange(-bs, nblk * bs, dtype=jnp.int32)
    nn = jnp.maximum(dist, 0)
    max_exact = REL_BUCKETS // 2
    nf = jnp.maximum(nn, max_exact).astype(F32)
    large = max_exact + (jnp.log(nf / max_exact) / math.log(REL_MAX_DIST / max_exact)
                         * (REL_BUCKETS - max_exact)).astype(jnp.int32)
    large = jnp.minimum(large, REL_BUCKETS - 1)
    bucket = jnp.where(nn < max_exact, nn, large)
    f = rel_bias.astype(F32)[bucket].T
    heads = f.shape[0]
    idx = (bs * jnp.arange(nblk)[:, None] + (bs - 1) - jnp.arange(2 * bs)[None, :]) + bs
    w = f[:, idx]
    rows = jnp.tile(w, (1, 1, bs))[:, :, :bs * (2 * bs - 1)].reshape(heads, nblk, bs, 2 * bs - 1)
    return rows[:, :, :, bs - 1:]


def kernel(x, c, positions, w_ada, b_ada, g_mix, w_in, g_qa, w_uq, g_kva, w_ukv, rel_bias,
           w_out, g_ffn, w_pq, peer_keys, peer_u, peer_v, g_final):
    bsz, seq, d = x.shape
    n = bsz * seq
    nblk = seq // MOBA_BLOCK
    x2 = x.reshape(n, d)
    assert w_ada.shape[0] == 1, "single-layer block: the final norm is fused into the peer kernel"
    for l in range(1):
        mod = _ada(c, w_ada[l], b_ada[l]).reshape(bsz, N_MOD, 1, d)
        sh_a, sc_a, gt_a, sh_f, sc_f, gt_f = (mod[:, j] for j in range(N_MOD))

        cq, ckv, kr, mq, mk, mv = jnp.split(w_in[l], np.cumsum(
            [MLA_Q_LORA, MLA_KV_LORA, MLA_ROPE, MOBA_HEADS * MOBA_HDIM, MOBA_HEADS * MOBA_HDIM])
            .tolist(), axis=1)
        kr_p = jnp.pad(kr, ((0, 0), (ROPE_LO, HEAD_PAD - ROPE_LO - MLA_ROPE)))
        w_in_p = jnp.concatenate(
            [cq, ckv, kr_p, _pad_heads(mq, MOBA_HEADS, MOBA_HDIM),
             _pad_heads(mk, MOBA_HEADS, MOBA_HDIM), mv], axis=1).astype(BF16)
        w_uq_p = _pad_heads(w_uq[l], MLA_HEADS, MLA_NOPE + MLA_ROPE).astype(BF16)
        w_kv = w_ukv[l].reshape(MLA_KV_LORA, MLA_HEADS, MLA_NOPE + MLA_VDIM)
        w_k_p = _pad_heads(w_kv[:, :, :MLA_NOPE].reshape(MLA_KV_LORA, -1), MLA_HEADS,
                           MLA_NOPE).astype(BF16)
        w_v = w_kv[:, :, MLA_NOPE:].reshape(MLA_KV_LORA, -1).astype(BF16)
        inv = ROPE_THETA ** (-jnp.arange(ROPE_HALF, dtype=F32) / ROPE_HALF)
        inv_row = jnp.zeros((1, LANE), F32).at[0, ROPE_LO:ROPE_LO + MLA_ROPE].set(
            jnp.concatenate([inv, inv]))
        pos = positions.reshape(n, 1)

        q_mla, k_mla, v_mla, q_mb, k_mb, v_mb = _inproj(
            x2, sc_a, sh_a, g_mix[l].reshape(1, d), w_in_p, g_qa[l].reshape(1, -1), w_uq_p,
            g_kva[l].reshape(1, -1), w_k_p, w_v, pos, inv_row, seq)

        o_mla = _attn(q_mla, k_mla, v_mla, seq)
        o_moba = _attn(q_mb, k_mb, v_mb, seq, _rel_bias_table(rel_bias, nblk))

        keys = peer_keys[l]
        zer = jnp.zeros_like(keys[:, 0])
        kpad = jnp.concatenate([jnp.concatenate([keys[:, 0], zer], axis=2),
                                jnp.concatenate([zer, keys[:, 1]], axis=2)], axis=1).astype(BF16)
        x1, h2, r1, e1, k0, e0 = _mid(
            x2, o_mla, o_moba, w_out[l].astype(BF16), gt_a, sc_f, sh_f, g_ffn[l].reshape(1, d),
            w_pq[l].astype(BF16), kpad, seq)

        vt = peer_v[l].astype(BF16).reshape(-1, PEER_SLAB, d).swapaxes(1, 2)
        x2 = _peer(h2, peer_u[l].astype(BF16), vt, r1, e1, k0, e0, x1,
                   gt_f, g_final.reshape(1, d), seq)
    return x2.reshape(bsz, seq, d)
```

```python
import functools
import math

import jax
import jax.numpy as jnp
import numpy as np
from jax import lax
from jax.experimental import pallas as pl
from jax.experimental.pallas import tpu as pltpu

F32 = jnp.float32
BF16 = jnp.bfloat16
HIGHEST = lax.Precision.HIGHEST

D_MODEL = 1024
N_MOD = 6
NORM_EPS = 1e-6
MLA_HEADS = 8
MLA_Q_LORA = 768
MLA_KV_LORA = 256
MLA_NOPE = 64
MLA_ROPE = 32
MLA_VDIM = 64
ROPE_THETA = 10000.0
MOBA_HEADS = 8
MOBA_HDIM = 64
MOBA_BLOCK = 256
MOBA_TOPK = 3
REL_BUCKETS = 32
REL_MAX_DIST = 1024
PEER_NKEYS = 128
PEER_HEADS = 8
PEER_TOPK = 16
PEER_HALF = 64

LANE = 128
HEAD_PAD = 128
ROPE_LO = MLA_NOPE
ROPE_HALF = MLA_ROPE // 2
MASK_NEG = -1e9
VMEM_LIMIT = 56 * 1024 * 1024
TOK_TILE = MOBA_BLOCK
MOBA_MAX_BLOCKS = LANE // MOBA_HEADS
ATTN_HEADS = 4
PEER_TM = 512
PEER_TE = 2048
PEER_SLAB = 256
PEER_TOK = 256
PEER_ITEMS = 4
BF16_ROWS = 16


def _cparams(sem, flags=None):
    return pltpu.CompilerParams(dimension_semantics=sem, vmem_limit_bytes=VMEM_LIMIT, flags=flags)


def _rms(x, g):
    ms = jnp.mean(x * x, axis=-1, keepdims=True)
    return x * lax.rsqrt(ms + NORM_EPS) * g


def _dot_nt(a, b, precision=None):
    return lax.dot_general(a, b, (((1,), (1,)), ((), ())), precision=precision,
                           preferred_element_type=F32)


def _ada_kernel(c_ref, w_ref, b_ref, o_ref):
    c = c_ref[...]
    s = c / (1.0 + jnp.exp(-c))
    o_ref[...] = jnp.dot(s, w_ref[...], precision=HIGHEST, preferred_element_type=F32) + b_ref[...]


def _ada(c, w, b):
    bsz, d = c.shape
    n = w.shape[1]
    return pl.pallas_call(
        _ada_kernel,
        out_shape=jax.ShapeDtypeStruct((bsz, n), F32),
        grid=(n // d,),
        in_specs=[pl.BlockSpec((bsz, d), lambda j: (0, 0)),
                  pl.BlockSpec((d, d), lambda j: (0, j)),
                  pl.BlockSpec((1, d), lambda j: (0, j))],
        out_specs=pl.BlockSpec((bsz, d), lambda j: (0, j)),
        compiler_params=_cparams(("arbitrary",)),
        name="ada",
    )(c, w, b.reshape(1, n))


def _rope_block(x, cosm, sinm, lane):
    up = pltpu.roll(x, LANE - ROPE_HALF, 1)
    dn = pltpu.roll(x, ROPE_HALF, 1)
    rot = jnp.where(lane < ROPE_LO + ROPE_HALF, -up, dn)
    return x * cosm + rot * sinm


def _inproj_kernel(blocks_per_seq,
                   x_ref, sc_ref, sh_ref, gmix_ref, win_ref, wmv_ref, gqa_ref, wuq_ref, gkva_ref,
                   wk_ref, wv_ref, pos_ref, inv_ref,
                   qmla_ref, kmla_ref, vmla_ref, mq_ref, mk_ref, mv_ref,
                   kmt_ref):
    i = pl.program_id(0)
    own = i % blocks_per_seq
    tm = x_ref.shape[0]

    @pl.when(i == 0)
    def _():
        kmt_ref[...] = jnp.zeros_like(kmt_ref)

    h = _rms(x_ref[...], gmix_ref[...]) * (1.0 + sc_ref[...]) + sh_ref[...]
    hb = h.astype(BF16)
    z = jnp.dot(hb, win_ref[...], preferred_element_type=F32)
    o = 0
    c_q = z[:, o:o + MLA_Q_LORA]; o += MLA_Q_LORA
    c_kv = z[:, o:o + MLA_KV_LORA]; o += MLA_KV_LORA
    k_r = z[:, o:o + HEAD_PAD]; o += HEAD_PAD
    mq = z[:, o:o + MOBA_HEADS * HEAD_PAD]; o += MOBA_HEADS * HEAD_PAD
    mk = z[:, o:o + MOBA_HEADS * HEAD_PAD]

    lane = lax.broadcasted_iota(jnp.int32, (tm, LANE), 1)
    in_rope = (lane >= ROPE_LO) & (lane < ROPE_LO + MLA_ROPE)
    ang = pos_ref[...].astype(F32) * inv_ref[...]
    cosm = jnp.where(in_rope, jnp.cos(ang), 1.0)
    sinm = jnp.where(in_rope, jnp.sin(ang), 0.0)

    qn = _rms(c_q, gqa_ref[...]).astype(BF16)
    q = jnp.dot(qn, wuq_ref[...], preferred_element_type=F32)
    q_scale = (MLA_NOPE + MLA_ROPE) ** -0.5
    kvn = _rms(c_kv, gkva_ref[...]).astype(BF16)
    kn = jnp.dot(kvn, wk_ref[...], preferred_element_type=F32)
    k_rr = _rope_block(k_r, cosm, sinm, lane)
    for hd in range(MLA_HEADS):
        sl = slice(hd * HEAD_PAD, (hd + 1) * HEAD_PAD)
        qmla_ref[:, sl] = (_rope_block(q[:, sl], cosm, sinm, lane) * q_scale).astype(BF16)
        kmla_ref[:, sl] = (kn[:, sl] + k_rr).astype(BF16)
    vmla_ref[...] = _dot_nt(wv_ref[...], kvn).astype(BF16)

    gate_t = _dot_nt(kmt_ref[...], mq, precision=HIGHEST)
    nblk = kmt_ref.shape[0] // MOBA_HEADS
    row = lax.broadcasted_iota(jnp.int32, (nblk, tm), 0).astype(F32)
    valid = row < own.astype(F32)
    bias_rows = []
    for hd in range(MOBA_HEADS):
        g = jnp.where(valid, gate_t[hd * nblk:(hd + 1) * nblk, :], -jnp.inf)
        bias = jnp.full((nblk, tm), MASK_NEG, F32)
        for _ in range(MOBA_TOPK):
            m = jnp.max(g, axis=0, keepdims=True)
            first = jnp.min(jnp.where(g == m, row, float(nblk)), axis=0, keepdims=True)
            pick = (row == first) & (m > -jnp.inf)
            bias = jnp.where(pick, 0.0, bias)
            g = jnp.where(pick, -jnp.inf, g)
        bias_rows.append(bias)
    sel_tm = jnp.concatenate(bias_rows, axis=0).T
    in_sel = (lane >= MOBA_HDIM) & (lane < MOBA_HDIM + nblk)
    m_scale = MOBA_HDIM ** -0.5
    for hd in range(MOBA_HEADS):
        sl = slice(hd * HEAD_PAD, (hd + 1) * HEAD_PAD)
        placed = pltpu.roll(sel_tm, (MOBA_HDIM - nblk * hd) % LANE, 1)
        mq_ref[:, sl] = jnp.where(in_sel, placed, mq[:, sl] * m_scale).astype(BF16)
    mk_ref[...] = mk.astype(BF16)
    mv_ref[...] = _dot_nt(wmv_ref[...], hb).astype(BF16)

    kmean = jnp.mean(mk, axis=0, keepdims=True)
    lane_k = lax.broadcasted_iota(jnp.int32, kmean.shape, 1)
    for hd in range(MOBA_HEADS):
        in_head = (lane_k >= hd * HEAD_PAD) & (lane_k < (hd + 1) * HEAD_PAD)
        kmt_ref[pl.ds(hd * nblk + own, 1), :] = jnp.where(in_head, kmean, 0.0)


def _inproj(x2, sc, sh, g_mix, w_in_p, w_mv_t, g_qa, w_uq_p, g_kva, w_k_p, w_v_t, pos, inv_row, seq):
    n, d = x2.shape
    tm = TOK_TILE
    bps = seq // tm
    assert seq // MOBA_BLOCK <= MOBA_MAX_BLOCKS
    full = lambda a: pl.BlockSpec(a.shape, lambda i: (0,) * a.ndim)
    per_b = pl.BlockSpec((None, 1, d), lambda i: (i // bps, 0, 0))
    tok = lambda w: pl.BlockSpec((tm, w), lambda i: (i, 0))
    tok_shape = lambda w: jax.ShapeDtypeStruct((n, w), BF16)
    vt_shape = lambda w: jax.ShapeDtypeStruct((n // tm, w, tm), BF16)
    vt_spec = lambda w: pl.BlockSpec((None, w, tm), lambda i: (i, 0, 0))
    qk_w = MLA_HEADS * HEAD_PAD
    return pl.pallas_call(
        functools.partial(_inproj_kernel, bps),
        out_shape=[tok_shape(qk_w), tok_shape(qk_w), vt_shape(MLA_HEADS * MLA_VDIM),
                   tok_shape(qk_w), tok_shape(qk_w), vt_shape(MOBA_HEADS * MOBA_HDIM)],
        grid=(n // tm,),
        in_specs=[tok(d), per_b, per_b, full(g_mix), full(w_in_p), full(w_mv_t), full(g_qa),
                  full(w_uq_p), full(g_kva), full(w_k_p), full(w_v_t), tok(1), full(inv_row)],
        out_specs=[tok(qk_w), tok(qk_w), vt_spec(MLA_HEADS * MLA_VDIM),
                   tok(qk_w), tok(qk_w), vt_spec(MOBA_HEADS * MOBA_HDIM)],
        scratch_shapes=[pltpu.VMEM((MOBA_HEADS * MOBA_MAX_BLOCKS, MOBA_HEADS * HEAD_PAD), F32)],
        compiler_params=_cparams(("arbitrary",)),
        name="inproj",
    )(x2, sc, sh, g_mix, w_in_p, w_mv_t, g_qa, w_uq_p, g_kva, w_k_p, w_v_t, pos, inv_row)


def _attn_kernel(moba, q_ref, k_ref, v_ref, *rest):
    if moba:
        f_ref, o_ref, t_ref = rest
    else:
        (o_ref,) = rest
    bq = pl.program_id(2)
    tq = q_ref.shape[0]
    nheads = q_ref.shape[1] // HEAD_PAD
    vd = v_ref.shape[1] // nheads

    if moba:
        @pl.when((pl.program_id(1) == 0) & (bq == 0))
        def _():
            for hh in range(nheads):
                for dd in range(t_ref.shape[1]):
                    win = jnp.broadcast_to(f_ref[hh:hh + 1, dd * tq:(dd + 2) * tq], (tq, 2 * tq))
                    t_ref[hh, dd] = pltpu.roll(win, 0, 1, stride=1, stride_axis=0)[:, tq:]

    key_i = lax.broadcasted_iota(jnp.int32, (tq, tq), 0)
    qry_i = lax.broadcasted_iota(jnp.int32, (tq, tq), 1)
    causal = key_i <= qry_i
    lane_k = lax.broadcasted_iota(jnp.int32, (1, HEAD_PAD), 1)

    def scores(hh, n, diag):
        sl = slice(hh * HEAD_PAD, (hh + 1) * HEAD_PAD)
        k = k_ref[pl.ds(pl.multiple_of(n * tq, tq), tq), sl]
        if moba and not diag:
            k = k + jnp.where(lane_k == MOBA_HDIM + n, 1.0, 0.0).astype(BF16)
        s = _dot_nt(k, q_ref[:, sl])
        if moba:
            s = s + t_ref[hh, bq - n]
        if diag:
            s = jnp.where(causal, s, MASK_NEG)
        return s

    def update(n0, nblocks, carry, diag):
        ss = [[scores(hh, n0 + j, diag) for j in range(nblocks)] for hh in range(nheads)]
        stats, probs = [], []
        for hh in range(nheads):
            m, l, _ = carry[hh]
            m_new = m
            for s in ss[hh]:
                m_new = jnp.maximum(m_new, jnp.max(s, axis=0, keepdims=True))
            alpha = jnp.exp(m - m_new)
            ps = [jnp.exp(s - m_new) for s in ss[hh]]
            l = alpha * l
            for p in ps:
                l = l + jnp.sum(p, axis=0, keepdims=True)
            stats.append((m_new, l, alpha))
            probs.append([p.astype(BF16) for p in ps])
        new = []
        for hh in range(nheads):
            m_new, l, alpha = stats[hh]
            acc = alpha * carry[hh][2]
            for j in range(nblocks):
                acc = acc + jnp.dot(v_ref[n0 + j, hh * vd:(hh + 1) * vd, :], probs[hh][j],
                                    preferred_element_type=F32)
            new.append((m_new, l, acc))
        return tuple(new)

    init = (jnp.full((1, tq), -jnp.inf, F32), jnp.zeros((1, tq), F32), jnp.zeros((vd, tq), F32))
    carry = update(bq, 1, (init,) * nheads, True)
    carry = lax.fori_loop(0, bq % 2, lambda n, c: update(n, 1, c, False), carry)
    carry = lax.fori_loop(0, bq // 2, lambda i, c: update(bq % 2 + 2 * i, 2, c, False), carry)
    out_t = jnp.concatenate([acc / l for (_, l, acc) in carry], axis=0)
    o_ref[...] = out_t.T.astype(o_ref.dtype)


def _attn(q, k, vt, seq, bias_by_dist=None):
    n = q.shape[0]
    bsz = n // seq
    heads = q.shape[1] // HEAD_PAD
    hs = ATTN_HEADS
    tq = MOBA_BLOCK
    nq = seq // tq
    vd = vt.shape[1] // heads
    moba = bias_by_dist is not None
    in_specs = [pl.BlockSpec((tq, hs * HEAD_PAD), lambda g, b, i: (b * nq + i, g)),
                pl.BlockSpec((seq, hs * HEAD_PAD), lambda g, b, i: (b, g)),
                pl.BlockSpec((nq, hs * vd, tq), lambda g, b, i: (b, g, 0))]
    args = [q, k, vt]
    scratch = []
    if moba:
        width = bias_by_dist.shape[1]
        in_specs.append(pl.BlockSpec((None, hs, width), lambda g, b, i: (g, 0, 0)))
        args.append(bias_by_dist.reshape(heads // hs, hs, width))
        scratch.append(pltpu.VMEM((hs, nq, tq, tq), F32))
    return pl.pallas_call(
        functools.partial(_attn_kernel, moba),
        out_shape=jax.ShapeDtypeStruct((n, heads * vd), BF16),
        grid=(heads // hs, bsz, nq),
        in_specs=in_specs,
        out_specs=pl.BlockSpec((tq, hs * vd), lambda g, b, i: (b * nq + i, g)),
        scratch_shapes=scratch,
        compiler_params=_cparams(("arbitrary", "arbitrary", "arbitrary")),
        name="moba_attn" if moba else "mla_attn",
    )(*args)


def _top16(s, v_ref, r_ref=None):
    r = jnp.full(s.shape, float(PEER_TOPK), F32)
    for a in range(PEER_TOPK):
        m = jnp.max(s, axis=0, keepdims=True)
        hit = s == m
        if r_ref is not None:
            r = jnp.where(hit, float(a), r)
        s = jnp.where(hit, -jnp.inf, s)
        v_ref[a:a + 1, :] = m
    if r_ref is not None:
        _store_lane_tiles(r_ref, r)


def _store_lane_tiles(ref, val):
    for lt in range(val.shape[1] // LANE):
        tile = val[:, lt * LANE:(lt + 1) * LANE]
        if ref.dtype == jnp.uint32:
            tile = pltpu.bitcast(tile.astype(BF16), jnp.uint32)
        ref[lt] = tile


def _mid_kernel(x_ref, oa_ref, ob_ref, wo_ref, gta_ref, scf_ref, shf_ref, gffn_ref, wpq_ref,
                kpad_ref, x1_ref, h2_ref, r1_ref, e1_ref, k0_ref, e0_ref, v0_s, v1_s):
    half = oa_ref.shape[1]
    y = (jnp.dot(oa_ref[...], wo_ref[:half, :], preferred_element_type=F32)
         + jnp.dot(ob_ref[...], wo_ref[half:, :], preferred_element_type=F32))
    x1 = x_ref[...] + gta_ref[...] * y
    x1_ref[...] = x1
    h2 = (_rms(x1, gffn_ref[...]) * (1.0 + scf_ref[...]) + shf_ref[...]).astype(BF16)
    h2_ref[...] = h2
    pq = jnp.dot(h2, wpq_ref[...], preferred_element_type=F32).astype(BF16)
    nk = PEER_NKEYS
    for hd in range(PEER_HEADS):
        st = _dot_nt(kpad_ref[hd], pq[:, hd * LANE:(hd + 1) * LANE])
        s0 = st[:nk, :]
        s1 = st[nk:, :]
        _top16(s0, v0_s)
        _top16(s1, v1_s, r1_ref.at[hd])
        v0 = v0_s[...]
        v1 = v1_s[...]
        groups = [v0[0:1, :] + v1[0:8, :], v0[0:1, :] + v1[8:16, :]]
        groups += [v0[a:a + 1, :] + v1[0:8, :] for a in range(1, 8)]
        groups.append(v0[8:16, :] + v1[0:1, :])
        top = groups[0][0:1, :]
        work = list(groups)
        thr = top
        for _ in range(PEER_TOPK):
            thr = jnp.max(functools.reduce(jnp.maximum, work), axis=0, keepdims=True)
            work = [jnp.where(g == thr, -jnp.inf, g) for g in work]
        z = functools.reduce(
            jnp.add, [jnp.sum(jnp.where(g >= thr, jnp.exp(g - top), 0.0), axis=0, keepdims=True)
                      for g in groups])
        cnt = jnp.zeros(s0.shape, F32)
        for b in range(PEER_TOPK):
            cnt = cnt + jnp.where(s0 + v1[b:b + 1, :] >= thr, 1.0, 0.0)
        _store_lane_tiles(k0_ref.at[hd], cnt)
        _store_lane_tiles(e0_ref.at[hd], jnp.exp(s0 - v0[0:1, :]) / z)
        _store_lane_tiles(e1_ref.at[hd], jnp.exp(s1 - v1[0:1, :]))


def _mid(x2, o_mla, o_moba, w_out, gta, scf, shf, g_ffn, w_pq, kpad, seq):
    n, d = x2.shape
    tm = TOK_TILE
    bps = seq // tm
    full = lambda a: pl.BlockSpec(a.shape, lambda i: (0,) * a.ndim)
    per_b = pl.BlockSpec((None, 1, d), lambda i: (i // bps, 0, 0))
    tok = lambda w: pl.BlockSpec((tm, w), lambda i: (i, 0))
    sel_shape = jax.ShapeDtypeStruct((PEER_HEADS, n // LANE, PEER_NKEYS, LANE), F32)
    sel_spec = pl.BlockSpec((PEER_HEADS, tm // LANE, PEER_NKEYS, LANE), lambda i: (0, i, 0, 0))
    pk_shape = jax.ShapeDtypeStruct((PEER_HEADS, n // LANE, PEER_NKEYS // 2, LANE), jnp.uint32)
    pk_spec = pl.BlockSpec((PEER_HEADS, tm // LANE, PEER_NKEYS // 2, LANE), lambda i: (0, i, 0, 0))
    return pl.pallas_call(
        _mid_kernel,
        out_shape=[jax.ShapeDtypeStruct((n, d), F32), jax.ShapeDtypeStruct((n, d), BF16),
                   pk_shape, pk_shape, sel_shape, sel_shape],
        grid=(n // tm,),
        in_specs=[tok(d), tok(o_mla.shape[1]), tok(o_moba.shape[1]), full(w_out), per_b, per_b,
                  per_b, full(g_ffn), full(w_pq), full(kpad)],
        out_specs=[tok(d), tok(d), pk_spec, pk_spec, sel_spec, sel_spec],
        scratch_shapes=[pltpu.VMEM((PEER_TOPK, tm), F32), pltpu.VMEM((PEER_TOPK, tm), F32)],
        compiler_params=_cparams(("arbitrary",)),
        name="mid",
    )(x2, o_mla, o_moba, w_out, gta, scf, shf, g_ffn, w_pq, kpad)


def _peer_kernel(h_ref, u_ref, vt_ref, r1_ref, e1_ref, k0_ref, e0_ref, x1_ref, gtf_ref, gfin_ref,
                 o_ref, acc_ref, ht_ref, at_ref, p_ref, kb_ref, eb_ref):
    e = pl.program_id(1)
    ltiles = h_ref.shape[0] // LANE
    te = u_ref.shape[0]
    ngroups = te // PEER_NKEYS
    tm = h_ref.shape[0]
    halves = tm // PEER_TOK
    htiles = PEER_TOK // LANE

    @pl.when(e == 0)
    def _():
        acc_ref[...] = jnp.zeros_like(acc_ref)
        ht_ref[...] = h_ref[...].T

    for hd in range(PEER_HEADS):
        for lt in range(ltiles):
            for ig in range(ngroups):
                idx = (hd * ltiles + lt) * ngroups + ig
                kb_ref[idx] = jnp.broadcast_to(
                    k0_ref[hd, lt, ig:ig + 1, :], (BF16_ROWS, LANE)).astype(BF16)
                eb_ref[idx] = jnp.broadcast_to(
                    e0_ref[hd, lt, ig:ig + 1, :], (BF16_ROWS, LANE)).astype(BF16)

    def expert_scores(sb, th, it):
        slab = pl.ds(pl.multiple_of(sb * PEER_SLAB, PEER_SLAB), PEER_SLAB)
        at = jnp.dot(u_ref[slab, :], ht_ref[:, th * PEER_TOK:(th + 1) * PEER_TOK],
                     preferred_element_type=F32)
        for lt in range(htiles):
            at_ref[it, lt] = at[:, lt * LANE:(lt + 1) * LANE]

    def expert_weights(sb, th, it):
        for g in range(PEER_SLAB // PEER_NKEYS):
            ig = sb * (PEER_SLAB // PEER_NKEYS) + g
            for lh in range(htiles):
                lt = th * htiles + lh
                k0 = [kb_ref[(hd * ltiles + lt) * ngroups + ig] for hd in range(PEER_HEADS)]
                e0 = [eb_ref[(hd * ltiles + lt) * ngroups + ig] for hd in range(PEER_HEADS)]
                for jb in range(PEER_NKEYS // BF16_ROWS):
                    wrows = slice(jb * BF16_ROWS // 2, (jb + 1) * BF16_ROWS // 2)
                    rows = slice(g * PEER_NKEYS + jb * BF16_ROWS,
                                 g * PEER_NKEYS + (jb + 1) * BF16_ROWS)
                    w = None
                    for hd in range(PEER_HEADS):
                        e1 = pltpu.bitcast(e1_ref[hd, lt, wrows, :], BF16)
                        r1 = pltpu.bitcast(r1_ref[hd, lt, wrows, :], BF16)
                        term = jnp.where(r1 < k0[hd], e1, jnp.zeros_like(e1)) * e0[hd]
                        w = term if w is None else w + term
                    a = at_ref[it, lh, rows, :]
                    gelu = 0.5 * a * (1.0 + lax.erf(a * (1.0 / math.sqrt(2.0))))
                    p_ref[it, rows, lh * LANE:(lh + 1) * LANE] = gelu.astype(BF16) * w

    def expert_out(sb, th, it):
        cols = slice(th * PEER_TOK, (th + 1) * PEER_TOK)
        acc_ref[:, cols] += jnp.dot(vt_ref[sb], p_ref[it], preferred_element_type=F32)

    slabs_per_step = PEER_ITEMS // halves

    def item_group(gi, carry):
        items = [(gi * slabs_per_step + it // halves, it % halves, it) for it in range(PEER_ITEMS)]
        for item in items:
            expert_scores(*item)
        for item in items:
            expert_weights(*item)
        for item in items:
            expert_out(*item)
        return carry

    lax.fori_loop(0, te // PEER_SLAB // slabs_per_step, item_group, 0)

    @pl.when(e == pl.num_programs(1) - 1)
    def _():
        x2 = x1_ref[...] + gtf_ref[...] * acc_ref[...].T
        o_ref[...] = _rms(x2, gfin_ref[...])


def _peer(h2, u_b, vt_b, r1, e1, k0, e0, x1, gtf, g_fin, seq):
    n, d = h2.shape
    ne = u_b.shape[0]
    tm, te = PEER_TM, PEER_TE
    bps = seq // tm
    sel_spec = pl.BlockSpec((PEER_HEADS, tm // LANE, PEER_NKEYS // 2, LANE),
                            lambda t, e: (0, t, 0, 0))
    key_spec = pl.BlockSpec((PEER_HEADS, tm // LANE, te // PEER_NKEYS, LANE),
                            lambda t, e: (0, t, e, 0))
    return pl.pallas_call(
        _peer_kernel,
        out_shape=jax.ShapeDtypeStruct((n, d), F32),
        grid=(n // tm, ne // te),
        in_specs=[pl.BlockSpec((tm, d), lambda t, e: (t, 0)),
                  pl.BlockSpec((te, d), lambda t, e: (e, 0)),
                  pl.BlockSpec((te // PEER_SLAB, d, PEER_SLAB), lambda t, e: (e, 0, 0)),
                  sel_spec, sel_spec, key_spec, key_spec,
                  pl.BlockSpec((tm, d), lambda t, e: (t, 0)),
                  pl.BlockSpec((None, 1, d), lambda t, e: (t // bps, 0, 0)),
                  pl.BlockSpec((1, d), lambda t, e: (0, 0))],
        out_specs=pl.BlockSpec((tm, d), lambda t, e: (t, 0)),
        scratch_shapes=[pltpu.VMEM((d, tm), F32), pltpu.VMEM((d, tm), BF16),
                        pltpu.VMEM((PEER_ITEMS, PEER_TOK // LANE, PEER_SLAB, LANE), F32),
                        pltpu.VMEM((PEER_ITEMS, PEER_SLAB, PEER_TOK), BF16)]
        + [pltpu.VMEM((PEER_HEADS * (tm // LANE) * (te // PEER_NKEYS), BF16_ROWS, LANE), BF16)] * 2,
        compiler_params=_cparams(("arbitrary", "arbitrary")),
        name="peer",
    )(h2, u_b, vt_b, r1, e1, k0, e0, x1, gtf, g_fin)


def _pad_heads(w, heads, width):
    k = w.shape[0]
    w = w.reshape(k, heads, width)
    return jnp.pad(w, ((0, 0), (0, 0), (0, HEAD_PAD - width))).reshape(k, heads * HEAD_PAD)


def _rel_bias_by_distance(rel_bias, nblk):
    bs = MOBA_BLOCK
    dist = jnp.arange(-bs, (nblk + 1) * bs, dtype=jnp.int32)
    nn = jnp.maximum(dist, 0)
    max_exact = REL_BUCKETS // 2
    nf = jnp.maximum(nn, max_exact).astype(F32)
    large = max_exact + (jnp.log(nf / max_exact) / math.log(REL_MAX_DIST / max_exact)
                         * (REL_BUCKETS - max_exact)).astype(jnp.int32)
    large = jnp.minimum(large, REL_BUCKETS - 1)
    bucket = jnp.where(nn < max_exact, nn, large)
    return rel_bias.astype(F32)[bucket].T


def kernel(x, c, positions, w_ada, b_ada, g_mix, w_in, g_qa, w_uq, g_kva, w_ukv, rel_bias,
           w_out, g_ffn, w_pq, peer_keys, peer_u, peer_v, g_final):
    bsz, seq, d = x.shape
    n = bsz * seq
    nblk = seq // MOBA_BLOCK
    x2 = x.reshape(n, d)
    assert w_ada.shape[0] == 1, "single-layer block: the final norm is fused into the peer kernel"
    for l in range(1):
        mod = _ada(c, w_ada[l], b_ada[l]).reshape(bsz, N_MOD, 1, d)
        sh_a, sc_a, gt_a, sh_f, sc_f, gt_f = (mod[:, j] for j in range(N_MOD))

        cq, ckv, kr, mq, mk, mv = jnp.split(w_in[l], np.cumsum(
            [MLA_Q_LORA, MLA_KV_LORA, MLA_ROPE, MOBA_HEADS * MOBA_HDIM, MOBA_HEADS * MOBA_HDIM])
            .tolist(), axis=1)
        kr_p = jnp.pad(kr, ((0, 0), (ROPE_LO, HEAD_PAD - ROPE_LO - MLA_ROPE)))
        w_in_p = jnp.concatenate(
            [cq, ckv, kr_p, _pad_heads(mq, MOBA_HEADS, MOBA_HDIM),
             _pad_heads(mk, MOBA_HEADS, MOBA_HDIM)], axis=1).astype(BF16)
        w_mv_t = mv.T.astype(BF16)
        w_uq_p = _pad_heads(w_uq[l], MLA_HEADS, MLA_NOPE + MLA_ROPE).astype(BF16)
        w_kv = w_ukv[l].reshape(MLA_KV_LORA, MLA_HEADS, MLA_NOPE + MLA_VDIM)
        w_k_p = _pad_heads(w_kv[:, :, :MLA_NOPE].reshape(MLA_KV_LORA, -1), MLA_HEADS,
                           MLA_NOPE).astype(BF16)
        w_v_t = w_kv[:, :, MLA_NOPE:].reshape(MLA_KV_LORA, -1).T.astype(BF16)
        inv = ROPE_THETA ** (-jnp.arange(ROPE_HALF, dtype=F32) / ROPE_HALF)
        inv_row = jnp.zeros((1, LANE), F32).at[0, ROPE_LO:ROPE_LO + MLA_ROPE].set(
            jnp.concatenate([inv, inv]))
        pos = positions.reshape(n, 1)

        q_mla, k_mla, v_mla, q_mb, k_mb, v_mb = _inproj(
            x2, sc_a, sh_a, g_mix[l].reshape(1, d), w_in_p, w_mv_t, g_qa[l].reshape(1, -1), w_uq_p,
            g_kva[l].reshape(1, -1), w_k_p, w_v_t, pos, inv_row, seq)

        o_mla = _attn(q_mla, k_mla, v_mla, seq)
        o_moba = _attn(q_mb, k_mb, v_mb, seq, _rel_bias_by_distance(rel_bias, nblk))

        keys = peer_keys[l]
        zer = jnp.zeros_like(keys[:, 0])
        kpad = jnp.concatenate([jnp.concatenate([keys[:, 0], zer], axis=2),
                                jnp.concatenate([zer, keys[:, 1]], axis=2)], axis=1).astype(BF16)
        x1, h2, r1, e1, k0, e0 = _mid(
            x2, o_mla, o_moba, w_out[l].astype(BF16), gt_a, sc_f, sh_f, g_ffn[l].reshape(1, d),
            w_pq[l].astype(BF16), kpad, seq)

        vt = peer_v[l].astype(BF16).reshape(-1, PEER_SLAB, d).swapaxes(1, 2)
        x2 = _peer(h2, peer_u[l].astype(BF16), vt, r1, e1, k0, e0, x1,
                   gt_f, g_final.reshape(1, d), seq)
    return x2.reshape(bsz, seq, d)
```
